```python
import jax
import jax.numpy as jnp
from jax import lax

D_MODEL = 1024
BATCH = 2
SEQ = 8192
DEPTH = 2

RWKV_HEAD_DIM = 64
RWKV_WIDTH = D_MODEL // 2
RWKV_HEADS = RWKV_WIDTH // RWKV_HEAD_DIM
DECAY_LORA = 64
AAA_LORA = 64
MV_LORA = 32
GATE_LORA = 128
LRU_WIDTH = D_MODEL // 2
LRU_BLOCKS = 8
LRU_BLOCK_DIM = LRU_WIDTH // LRU_BLOCKS
CONV_WIDTH = 4
LRU_C = 8.0
D_FF = 2816
N_NORMS = 6
RMS_EPS = 1e-6
GN_EPS = 64e-5
W_SHIFT = 3 * RWKV_WIDTH + DECAY_LORA + AAA_LORA + GATE_LORA
W_IN = W_SHIFT + 2 * LRU_WIDTH + 2 * D_MODEL

kernel_name = "hybrid_rwkv7_rglru_macaron_block"


def _rmsnorm(x, g):
    x32 = x.astype(jnp.float32)
    y = x32 * lax.rsqrt(jnp.mean(x32 * x32, axis=-1, keepdims=True) + RMS_EPS)
    return (y * g.astype(jnp.float32)).astype(x.dtype)


def _swiglu(h, w_up, w_down):
    gate, up = jnp.split(h @ w_up, 2, axis=-1)
    return (jax.nn.silu(gate) * up) @ w_down


def _token_shift(p):
    return jnp.pad(p[:, :-1], ((0, 0), (1, 0), (0, 0)))


def _heads(t):
    return t.reshape(t.shape[0], t.shape[1], RWKV_HEADS, RWKV_HEAD_DIM)


def _rwkv7_scan(r, w, k, v, a, b):
    bsz = r.shape[0]
    xs = tuple(jnp.moveaxis(t, 1, 0) for t in (r, w, k, v, a, b))

    def step(S, inp):
        r_t, w_t, k_t, v_t, a_t, b_t = inp
        sa = jnp.einsum("bhvk,bhk->bhv", S, a_t)
        S = (S * w_t[:, :, None, :] + sa[..., :, None] * b_t[..., None, :]
             + v_t[..., :, None] * k_t[..., None, :])
        return S, jnp.einsum("bhvk,bhk->bhv", S, r_t)

    S0 = jnp.zeros((bsz, RWKV_HEADS, RWKV_HEAD_DIM, RWKV_HEAD_DIM), jnp.float32)
    _, y = lax.scan(step, S0, xs)
    return jnp.moveaxis(y, 0, 1)


def _group_norm(y, gain, bias):
    mu = jnp.mean(y, axis=-1, keepdims=True)
    var = jnp.mean(jnp.square(y - mu), axis=-1, keepdims=True)
    yn = (y - mu) * lax.rsqrt(var + GN_EPS)
    return yn.reshape(y.shape[0], y.shape[1], RWKV_WIDTH) * gain + bias


def _rwkv7_branch(p, v_first, v_mix, w0, w_up, a0, a_up, g_up, k_k, k_a, r_k, ln_w, ln_b):
    p = p.astype(jnp.float32)
    r, k, v, xw, xa, xg = jnp.split(
        p, [RWKV_WIDTH, 2 * RWKV_WIDTH, 3 * RWKV_WIDTH, 3 * RWKV_WIDTH + DECAY_LORA,
            3 * RWKV_WIDTH + DECAY_LORA + AAA_LORA], axis=-1)
    w = -jax.nn.softplus(-(w0 + jnp.tanh(xw) @ w_up)) - 0.5
    if v_first is None:
        v_first = v
    else:
        v0, v_down, v_upm = v_mix
        v = v + (v_first - v) * jax.nn.sigmoid(v0 + (v @ v_down) @ v_upm)
    a = jax.nn.sigmoid(a0 + xa @ a_up)
    g = jax.nn.sigmoid(xg) @ g_up
    kk = _heads(k * k_k)
    kk = kk / jnp.maximum(jnp.sqrt(jnp.sum(kk * kk, axis=-1, keepdims=True)), 1e-12)
    k = k * (1.0 + (a - 1.0) * k_a)
    rh, kh, vh = _heads(r), _heads(k), _heads(v)
    y = _rwkv7_scan(rh, _heads(jnp.exp(-jnp.exp(w))), kh, vh, -kk, kk * _heads(a))
    bonus = jnp.sum(rh * kh * r_k.reshape(RWKV_HEADS, RWKV_HEAD_DIM), axis=-1, keepdims=True) * vh
    y = _group_norm(y, ln_w, ln_b) + bonus.reshape(y.shape[0], y.shape[1], RWKV_WIDTH)
    return y * g, v_first


def _rglru_branch(xb, gate_in, conv_w, conv_b, lru_wa, lru_ba, lru_wx, lru_bx, lru_lambda):
    bsz, seq = xb.shape[0], xb.shape[1]
    xb = lax.conv_general_dilated(
        xb, conv_w[:, None, :], window_strides=(1,), padding=[(CONV_WIDTH - 1, 0)],
        dimension_numbers=("NWC", "WIO", "NWC"), feature_group_count=LRU_WIDTH) + conv_b
    blocks = xb.reshape(bsz, seq, LRU_BLOCKS, LRU_BLOCK_DIM)
    gate_a = jax.nn.sigmoid(jnp.einsum("btnd,nde->btne", blocks, lru_wa).reshape(bsz, seq, LRU_WIDTH) + lru_ba)
    gate_x = jax.nn.sigmoid(jnp.einsum("btnd,nde->btne", blocks, lru_wx).reshape(bsz, seq, LRU_WIDTH) + lru_bx)
    log_a = -LRU_C * gate_a.astype(jnp.float32) * jax.nn.softplus(-lru_lambda.astype(jnp.float32))
    a = jnp.exp(log_a)
    mult = jnp.sqrt(-jnp.expm1(2.0 * log_a))
    mult = jnp.where(jnp.arange(seq)[None, :, None] == 0, 1.0, mult)
    b = (xb * gate_x).astype(jnp.float32) * mult

    def combine(c1, c2):
        a1, b1 = c1
        a2, b2 = c2
        return a1 * a2, a2 * b1 + b2

    _, h = lax.associative_scan(combine, (a, b), axis=1)
    return h.astype(xb.dtype) * jax.nn.gelu(gate_in)


def _hybrid_mixer(h, v_first, v_mix, w_in, shift_mu, w0, w_up, a0, a_up, g_up, k_k, k_a, r_k,
                  ln_w, ln_b, conv_w, conv_b, lru_wa, lru_ba, lru_wx, lru_bx, lru_lambda,
                  p_rwkv, p_lru, w_out):
    proj = h @ w_in
    p_shift, lru_x, lru_gate, gate_r, gate_l = jnp.split(
        proj, [W_SHIFT, W_SHIFT + LRU_WIDTH, W_SHIFT + 2 * LRU_WIDTH,
               W_SHIFT + 2 * LRU_WIDTH + D_MODEL], axis=-1)
    p_shift = p_shift + shift_mu * (_token_shift(p_shift) - p_shift)
    o_a, v_first = _rwkv7_branch(p_shift, v_first, v_mix, w0, w_up, a0, a_up, g_up,
                                 k_k, k_a, r_k, ln_w, ln_b)
    o_b = _rglru_branch(lru_x, lru_gate, conv_w, conv_b, lru_wa, lru_ba, lru_wx, lru_bx, lru_lambda)
    y = (jax.nn.sigmoid(gate_r) * (o_a.astype(h.dtype) @ p_rwkv)
         + jax.nn.sigmoid(gate_l) * (o_b @ p_lru))
    return y @ w_out, v_first


def setup_inputs(seed: int = 0) -> dict:
    key = jax.random.key(seed)
    ks = jax.random.split(key, 32)
    f32 = jnp.float32

    def nrm(k, shape, fan_in, scale=1.0):
        return jax.random.normal(k, shape, f32) * (scale * fan_in ** -0.5)

    L, D, RW, LW = DEPTH, D_MODEL, RWKV_WIDTH, LRU_WIDTH
    u = jax.random.uniform(ks[29], (L, LW), f32, minval=0.9, maxval=0.999)
    s = u ** (1.0 / LRU_C)
    return {
        "x": jax.random.normal(ks[0], (BATCH, SEQ, D), f32),
        "norm_g": 1.0 + 0.02 * jax.random.normal(ks[1], (L, N_NORMS, D), f32),
        "ffn1_w_up": nrm(ks[2], (L, D, 2 * D_FF), D),
        "ffn1_w_down": nrm(ks[3], (L, D_FF, D), D_FF),
        "ffn2_w_up": nrm(ks[4], (L, D, 2 * D_FF), D),
        "ffn2_w_down": nrm(ks[5], (L, D_FF, D), D_FF),
        "w_in": nrm(ks[6], (L, D, W_IN), D),
        "shift_mu": jax.random.uniform(ks[7], (L, W_SHIFT), f32),
        "w0": jax.random.uniform(ks[8], (L, RW), f32, minval=-6.0, maxval=1.0),
        "w_up": nrm(ks[9], (L, DECAY_LORA, RW), DECAY_LORA, 0.5),
        "a0": 0.1 * jax.random.normal(ks[10], (L, RW), f32),
        "a_up": nrm(ks[11], (L, AAA_LORA, RW), AAA_LORA),
        "g_up": nrm(ks[12], (L, GATE_LORA, RW), GATE_LORA),
        "k_k": 0.85 + 0.05 * jax.random.normal(ks[13], (L, RW), f32),
        "k_a": 1.0 + 0.05 * jax.random.normal(ks[14], (L, RW), f32),
        "r_k": 0.1 * jax.random.normal(ks[15], (L, RW), f32),
        "ln_w": 1.0 + 0.02 * jax.random.normal(ks[16], (L, RW), f32),
        "ln_b": 0.02 * jax.random.normal(ks[17], (L, RW), f32),
        "v0": 0.5 + 0.1 * jax.random.normal(ks[18], (L - 1, RW), f32),
        "v_down": nrm(ks[19], (L - 1, RW, MV_LORA), RW),
        "v_up": nrm(ks[20], (L - 1, MV_LORA, RW), MV_LORA),
        "conv_w": nrm(ks[21], (L, CONV_WIDTH, LW), CONV_WIDTH),
        "conv_b": 0.02 * jax.random.normal(ks[22], (L, LW), f32),
        "lru_wa": nrm(ks[23], (L, LRU_BLOCKS, LRU_BLOCK_DIM, LRU_BLOCK_DIM), LRU_BLOCK_DIM),
        "lru_ba": 0.02 * jax.random.normal(ks[24], (L, LW), f32),
        "lru_wx": nrm(ks[25], (L, LRU_BLOCKS, LRU_BLOCK_DIM, LRU_BLOCK_DIM), LRU_BLOCK_DIM),
        "lru_bx": 0.02 * jax.random.normal(ks[26], (L, LW), f32),
        "lru_lambda": jnp.log(s) - jnp.log1p(-s),
        "p_rwkv": nrm(ks[27], (L, RW, D), RW),
        "p_lru": nrm(ks[28], (L, LW, D), LW),
        "w_out": nrm(ks[30], (L, D, D), D),
    }


def reference(x, norm_g, ffn1_w_up, ffn1_w_down, ffn2_w_up, ffn2_w_down, w_in, shift_mu,
              w0, w_up, a0, a_up, g_up, k_k, k_a, r_k, ln_w, ln_b, v0, v_down, v_up,
              conv_w, conv_b, lru_wa, lru_ba, lru_wx, lru_bx, lru_lambda, p_rwkv, p_lru, w_out):
    v_first = None
    for l in range(DEPTH):
        g = norm_g[l]
        h = _swiglu(_rmsnorm(x, g[0]), ffn1_w_up[l], ffn1_w_down[l])
        x = x + 0.5 * _rmsnorm(h, g[1])
        v_mix = None if l == 0 else (v0[l - 1], v_down[l - 1], v_up[l - 1])
        h, v_first = _hybrid_mixer(
            _rmsnorm(x, g[2]), v_first, v_mix, w_in[l], shift_mu[l], w0[l], w_up[l], a0[l],
            a_up[l], g_up[l], k_k[l], k_a[l], r_k[l], ln_w[l], ln_b[l], conv_w[l], conv_b[l],
            lru_wa[l], lru_ba[l], lru_wx[l], lru_bx[l], lru_lambda[l], p_rwkv[l], p_lru[l], w_out[l])
        x = x + _rmsnorm(h, g[3])
        h = _swiglu(_rmsnorm(x, g[4]), ffn2_w_up[l], ffn2_w_down[l])
        x = x + 0.5 * _rmsnorm(h, g[5])
    return x
```

```python
import functools

import jax
import jax.numpy as jnp
from jax import lax
from jax.experimental import pallas as pl
from jax.experimental.pallas import tpu as pltpu

D_MODEL = 1024
RWKV_WIDTH = 512
HEAD_DIM = 64
LRU_WIDTH = 512
LRU_BLOCKS = 8
CONV_WIDTH = 4
LRU_C = 8.0
D_FF = 2816
RMS_EPS = 1e-6
GN_EPS = 64e-5
DECAY_LORA = 64
AAA_LORA = 64
GATE_LORA = 128
W_SHIFT = 3 * RWKV_WIDTH + DECAY_LORA + AAA_LORA + GATE_LORA
W_IN = W_SHIFT + 2 * LRU_WIDTH + 2 * D_MODEL

CHUNK = 64
LANES = 128
SUBLANES = 8
FFN_TILE = 512
FFN_COLS = 256
MIX_TILE = 256
VMEM_LIMIT = 56 * 1024 * 1024

BF16 = jnp.bfloat16
F32 = jnp.float32


def _dot(a, b):
    return jnp.dot(a.astype(BF16), b.astype(BF16), preferred_element_type=F32)


def _dot_nt(a, b):
    return lax.dot_general(a.astype(BF16), b.astype(BF16), (((1,), (1,)), ((), ())),
                           preferred_element_type=F32)


def _dot_tn(a, b):
    return lax.dot_general(a.astype(BF16), b.astype(BF16), (((0,), (0,)), ((), ())),
                           preferred_element_type=F32)


def _dot_hilo(a, b_bf16):
    hi = a.astype(BF16)
    lo = (a - hi.astype(F32)).astype(BF16)
    return (jnp.dot(hi, b_bf16, preferred_element_type=F32)
            + jnp.dot(lo, b_bf16, preferred_element_type=F32))


def _rmsnorm(x, g):
    ms = jnp.mean(x * x, axis=-1, keepdims=True)
    return x * lax.rsqrt(ms + RMS_EPS) * g


def _softplus(x):
    return jnp.maximum(x, 0.0) + jnp.log1p(jnp.exp(-jnp.abs(x)))


def _shift_rows(prev_rows, x, d):
    ext = jnp.concatenate([prev_rows, x], axis=0)
    return pltpu.roll(ext, d, 0)[SUBLANES:]


def _full(shape):
    return pl.BlockSpec(shape, lambda *_: (0,) * len(shape))


def _ffn_kernel(x_ref, g_ref, wup_ref, wdn_ref, o_ref):
    x = x_ref[...]
    h = _rmsnorm(x, g_ref[0:1, :]).astype(BF16)
    acc = jnp.zeros(x.shape, F32)
    for j in range(D_FF // FFN_COLS):
        lo = j * FFN_COLS
        gate = jnp.dot(h, wup_ref[:, lo:lo + FFN_COLS], preferred_element_type=F32)
        up = jnp.dot(h, wup_ref[:, D_FF + lo:D_FF + lo + FFN_COLS], preferred_element_type=F32)
        act = (gate * jax.nn.sigmoid(gate) * up).astype(BF16)
        acc = acc + jnp.dot(act, wdn_ref[lo:lo + FFN_COLS, :], preferred_element_type=F32)
    o_ref[...] = x + 0.5 * _rmsnorm(acc, g_ref[1:2, :])


def _ffn(x, g2, w_up, w_down):
    n = x.shape[0]
    return pl.pallas_call(
        _ffn_kernel,
        grid=(n // FFN_TILE,),
        in_specs=[
            pl.BlockSpec((FFN_TILE, D_MODEL), lambda i: (i, 0)),
            _full((2, D_MODEL)),
            _full((D_MODEL, 2 * D_FF)),
            _full((D_FF, D_MODEL)),
        ],
        out_specs=pl.BlockSpec((FFN_TILE, D_MODEL), lambda i: (i, 0)),
        out_shape=jax.ShapeDtypeStruct((n, D_MODEL), F32),
        compiler_params=pltpu.CompilerParams(
            dimension_semantics=("arbitrary",), vmem_limit_bytes=VMEM_LIMIT),
        name="ffn",
    )(x, g2, w_up, w_down)


def _scan_rows(a, b):
    n = a.shape[0]
    row = lax.broadcasted_iota(jnp.int32, a.shape, 0)
    d = 1
    while d < n:
        keep = row >= d
        a_prev = jnp.where(keep, pltpu.roll(a, d, 0), 1.0)
        b_prev = jnp.where(keep, pltpu.roll(b, d, 0), 0.0)
        b = a * b_prev + b
        a = a * a_prev
        d *= 2
    return a, b


def _mixer_in_kernel(has_vmix, tiles_per_seq, *refs):
    (x_ref, g_ref, win_ref, mu_ref, w0_ref, wup_ref, a0_ref, aup_ref, gup_ref,
     kk_ref, ka_ref, rk_ref) = refs[:12]
    refs = refs[12:]
    if has_vmix:
        v0_ref, vdown_ref, vup_ref, vfirst_ref = refs[:4]
        refs = refs[4:]
    (convw_ref, convb_ref, wax_ref, bax_ref, lam_ref, hsum_ref,
     r_out, lw_out, k_out, v_out, kkn_out, a_out, bonus_out, g_out, ob_out, sr_out, sl_out,
     pcarry, xcarry, hcarry) = refs

    tm = x_ref.shape[0]
    first = (pl.program_id(0) % tiles_per_seq) == 0

    @pl.when(first)
    def _():
        pcarry[...] = jnp.zeros(pcarry.shape, F32)
        xcarry[...] = jnp.zeros(xcarry.shape, F32)
        hcarry[...] = jnp.zeros(hcarry.shape, F32)

    h = _rmsnorm(x_ref[...], g_ref[...])
    proj = _dot(h, win_ref[...])

    p = proj[:, :W_SHIFT]
    prev = _shift_rows(pcarry[...], p, 1)
    pcarry[...] = p[tm - SUBLANES:, :]
    p = p + mu_ref[...] * (prev - p)
    rw = RWKV_WIDTH
    r, k, v = p[:, 0:rw], p[:, rw:2 * rw], p[:, 2 * rw:3 * rw]
    xwa = p[:, 3 * rw:3 * rw + DECAY_LORA + AAA_LORA]
    xg = p[:, 3 * rw + DECAY_LORA + AAA_LORA:]
    w = -_softplus(-(w0_ref[...] + _dot(jnp.tanh(xwa), wup_ref[...]))) - 0.5
    lw_out[...] = -jnp.exp(w)
    a = jax.nn.sigmoid(a0_ref[...] + _dot(xwa, aup_ref[...]))
    g_out[...] = _dot(jax.nn.sigmoid(xg), gup_ref[...])
    if has_vmix:
        mix = jax.nn.sigmoid(v0_ref[...] + _dot(_dot(v, vdown_ref[...]), vup_ref[...]))
        v = v + (vfirst_ref[...] - v) * mix
    kk = k * kk_ref[...]
    norm = jnp.sqrt(_dot_hilo(kk * kk, hsum_ref[...]))
    kk = kk / jnp.maximum(norm, 1e-12)
    k = k * (1.0 + (a - 1.0) * ka_ref[...])
    r_out[...] = r
    k_out[...] = k
    v_out[...] = v
    kkn_out[...] = kk
    a_out[...] = a
    bonus_out[...] = _dot_hilo(r * k * rk_ref[...], hsum_ref[...]) * v

    lw_ = LRU_WIDTH
    lx = proj[:, W_SHIFT:W_SHIFT + lw_]
    lgate = proj[:, W_SHIFT + lw_:W_SHIFT + 2 * lw_]
    sr_out[...] = jax.nn.sigmoid(proj[:, W_SHIFT + 2 * lw_:W_SHIFT + 2 * lw_ + D_MODEL])
    sl_out[...] = jax.nn.sigmoid(proj[:, W_SHIFT + 2 * lw_ + D_MODEL:])
    xprev = xcarry[...]
    xb = convb_ref[...] + convw_ref[CONV_WIDTH - 1:CONV_WIDTH, :] * lx
    for d in range(1, CONV_WIDTH):
        xb = xb + convw_ref[CONV_WIDTH - 1 - d:CONV_WIDTH - d, :] * _shift_rows(xprev, lx, d)
    xcarry[...] = lx[tm - SUBLANES:, :]
    gates = _dot(xb, wax_ref[...]) + bax_ref[...]
    gate_a = jax.nn.sigmoid(gates[:, :lw_])
    gate_x = jax.nn.sigmoid(gates[:, lw_:])
    log_a = -LRU_C * gate_a * _softplus(-lam_ref[...])
    a_l = jnp.exp(log_a)
    mult = jnp.sqrt(-jnp.tanh(log_a) * (a_l * a_l + 1.0))
    row = lax.broadcasted_iota(jnp.int32, log_a.shape, 0)
    seq_start = row + jnp.where(first, 0, 1) == 0
    mult = jnp.where(seq_start, 1.0, mult)
    b_l = xb * gate_x * mult
    a_cum, b_cum = _scan_rows(a_l, b_l)
    hseq = a_cum * hcarry[SUBLANES - 1:SUBLANES, :] + b_cum
    hcarry[...] = hseq[tm - SUBLANES:, :]
    ob_out[...] = hseq * jax.nn.gelu(lgate)


def _mixer_in(x, seq, g, w_in, mu, w0, wup_pad, a0, aup_pad, g_up, k_k, k_a, r_k, vmix,
              conv_w, conv_b, wax, bax, lam, hsum):
    n = x.shape[0]
    tm = MIX_TILE
    has_vmix = vmix is not None
    row512 = pl.BlockSpec((tm, RWKV_WIDTH), lambda i: (i, 0))
    row1024 = pl.BlockSpec((tm, D_MODEL), lambda i: (i, 0))
    lora = DECAY_LORA + AAA_LORA
    in_specs = [
        row1024, _full((1, D_MODEL)), _full((D_MODEL, W_IN)), _full((1, W_SHIFT)),
        _full((1, RWKV_WIDTH)), _full((lora, RWKV_WIDTH)), _full((1, RWKV_WIDTH)),
        _full((lora, RWKV_WIDTH)), _full((GATE_LORA, RWKV_WIDTH)),
        _full((1, RWKV_WIDTH)), _full((1, RWKV_WIDTH)), _full((1, RWKV_WIDTH)),
    ]
    args = [x, g, w_in, mu, w0, wup_pad, a0, aup_pad, g_up, k_k, k_a, r_k]
    if has_vmix:
        v0, v_down, v_up, v_first = vmix
        in_specs += [_full((1, RWKV_WIDTH)), _full(v_down.shape), _full(v_up.shape), row512]
        args += [v0, v_down, v_up, v_first]
    in_specs += [
        _full((CONV_WIDTH, LRU_WIDTH)), _full((1, LRU_WIDTH)), _full((LRU_WIDTH, 2 * LRU_WIDTH)),
        _full((1, 2 * LRU_WIDTH)), _full((1, LRU_WIDTH)), _full((RWKV_WIDTH, RWKV_WIDTH)),
    ]
    args += [conv_w, conv_b, wax, bax, lam, hsum]
    out512 = jax.ShapeDtypeStruct((n, RWKV_WIDTH), F32)
    out1024 = jax.ShapeDtypeStruct((n, D_MODEL), F32)
    return pl.pallas_call(
        functools.partial(_mixer_in_kernel, has_vmix, seq // tm),
        grid=(n // tm,),
        in_specs=in_specs,
        out_specs=[row512] * 9 + [row1024] * 2,
        out_shape=[out512] * 9 + [out1024] * 2,
        scratch_shapes=[
            pltpu.VMEM((SUBLANES, W_SHIFT), F32),
            pltpu.VMEM((SUBLANES, LRU_WIDTH), F32),
            pltpu.VMEM((SUBLANES, LRU_WIDTH), F32),
        ],
        compiler_params=pltpu.CompilerParams(
            dimension_semantics=("arbitrary",), vmem_limit_bytes=VMEM_LIMIT),
        name="mixer_in",
    )(*args)


def _cumsum_rows(x):
    n = x.shape[0]
    row = lax.broadcasted_iota(jnp.int32, x.shape, 0)
    d = 1
    while d < n:
        x = x + jnp.where(row >= d, pltpu.roll(x, d, 0), 0.0)
        d *= 2
    return x


def _rwkv_kernel(r_ref, lw_ref, k_ref, v_ref, kk_ref, a_ref, y_ref, s_ref):
    c = CHUNK

    @pl.when(pl.program_id(1) == 0)
    def _():
        s_ref[...] = jnp.zeros(s_ref.shape, F32)

    lw = lw_ref[...]
    cum = _cumsum_rows(lw)
    last = cum[c - 1:c, :]
    kk = kk_ref[...]
    k = k_ref[...]
    b = kk * a_ref[...]
    e_neg = jnp.exp(-cum)
    e_tail = jnp.exp(last - cum)
    a_t = -kk * jnp.exp(cum - lw)
    r_t = r_ref[...] * jnp.exp(cum)
    b_t = b * e_neg
    k_t = k * e_neg
    b_h = b * e_tail
    k_h = k * e_tail
    w_c = jnp.exp(last)
    v = v_ref[...]

    lane = lax.broadcasted_iota(jnp.int32, (c, LANES), 1)
    t_idx = lax.broadcasted_iota(jnp.int32, (c, LANES), 0)
    left = lane < HEAD_DIM
    s_idx = jnp.where(left, lane, lane - HEAD_DIM)
    strict = t_idx > s_idx
    incl = t_idx >= s_idx
    eye_w = (t_idx == s_idx).astype(F32)
    row2 = lax.broadcasted_iota(jnp.int32, (LANES, LANES), 0)
    lane2 = lax.broadcasted_iota(jnp.int32, (LANES, LANES), 1)
    same_head = (row2 < HEAD_DIM) == (lane2 < HEAD_DIM)
    eye2 = row2 == lane2

    def bd(w):
        return jnp.concatenate([jnp.where(left, w, 0.0), jnp.where(left, 0.0, w)], axis=0)

    for p in range(RWKV_WIDTH // LANES):
        sl = slice(p * LANES, (p + 1) * LANES)
        a_p, r_p, v_p = a_t[:, sl], r_t[:, sl], v[:, sl]
        m = _dot_nt(jnp.concatenate([a_p, r_p], axis=0),
                    jnp.concatenate([bd(b_t[:, sl]), bd(k_t[:, sl])], axis=0))
        a_ab = jnp.where(strict, m[:c, :LANES], 0.0)
        a_ak = jnp.where(strict, m[:c, LANES:], 0.0)
        a_rb = jnp.where(incl, m[c:, :LANES], 0.0)
        a_rk = jnp.where(incl, m[c:, LANES:], 0.0)
        t_w = eye_w + a_ab
        p_w = _dot(a_ab, bd(a_ab))
        steps = CHUNK.bit_length() - 2
        for i in range(steps):
            p_bd = bd(p_w)
            if i < steps - 1:
                prod = _dot(jnp.concatenate([t_w, p_w], axis=0), p_bd)
                t_w = t_w + prod[:c]
                p_w = prod[c:]
            else:
                t_w = t_w + _dot(t_w, p_bd)
        s_old = s_ref[p]
        v_bd = bd(v_p)
        x = _dot(jnp.concatenate([a_p, a_ak], axis=1), jnp.concatenate([s_old, v_bd], axis=0))
        u = _dot(t_w, bd(x))
        y_ref[:, sl] = _dot(jnp.concatenate([r_p, a_rb, a_rk], axis=1),
                            jnp.concatenate([s_old, bd(u), v_bd], axis=0))
        ds = _dot_tn(jnp.concatenate([b_h[:, sl], k_h[:, sl]], axis=0),
                     jnp.concatenate([u, v_p], axis=0))
        w_col = jnp.sum(jnp.where(eye2, w_c[:, sl], 0.0), axis=1, keepdims=True)
        s_ref[p] = w_col * s_old + jnp.where(same_head, ds, 0.0)


def _rwkv(r, lw, k, v, kk, a, batch, seq):
    n = r.shape[0]
    chunks = seq // CHUNK
    spec = pl.BlockSpec((CHUNK, RWKV_WIDTH), lambda b, c: (b * chunks + c, 0))
    return pl.pallas_call(
        _rwkv_kernel,
        grid=(batch, chunks),
        in_specs=[spec] * 6,
        out_specs=spec,
        out_shape=jax.ShapeDtypeStruct((n, RWKV_WIDTH), F32),
        scratch_shapes=[pltpu.VMEM((RWKV_WIDTH // LANES, LANES, LANES), F32)],
        compiler_params=pltpu.CompilerParams(
            dimension_semantics=("arbitrary", "arbitrary"), vmem_limit_bytes=VMEM_LIMIT),
        name="rwkv",
    )(r, lw, k, v, kk, a)


def _mixer_out_kernel(x_ref, y_ref, bonus_ref, g_ref, ob_ref, sr_ref, sl_ref, lnw_ref, lnb_ref,
                      prw_ref, plr_ref, wout_ref, gn_ref, hmean_ref, o_ref):
    y = y_ref[...]
    mu = _dot_hilo(y, hmean_ref[...])
    yc = y - mu
    var = _dot_hilo(yc * yc, hmean_ref[...])
    yn = yc * lax.rsqrt(var + GN_EPS) * lnw_ref[...] + lnb_ref[...] + bonus_ref[...]
    o_a = yn * g_ref[...]
    merged = sr_ref[...] * _dot(o_a, prw_ref[...]) + sl_ref[...] * _dot(ob_ref[...], plr_ref[...])
    out = _dot(merged, wout_ref[...])
    o_ref[...] = x_ref[...] + _rmsnorm(out, gn_ref[...])


def _mixer_out(x, y, bonus, g, o_b, sr, sl, ln_w, ln_b, p_rwkv, p_lru, w_out, gn, hmean):
    n = x.shape[0]
    tm = MIX_TILE
    row512 = pl.BlockSpec((tm, RWKV_WIDTH), lambda i: (i, 0))
    row1024 = pl.BlockSpec((tm, D_MODEL), lambda i: (i, 0))
    return pl.pallas_call(
        _mixer_out_kernel,
        grid=(n // tm,),
        in_specs=[row1024, row512, row512, row512, row512, row1024, row1024,
                  _full((1, RWKV_WIDTH)), _full((1, RWKV_WIDTH)),
                  _full((RWKV_WIDTH, D_MODEL)), _full((LRU_WIDTH, D_MODEL)),
                  _full((D_MODEL, D_MODEL)), _full((1, D_MODEL)),
                  _full((RWKV_WIDTH, RWKV_WIDTH))],
        out_specs=row1024,
        out_shape=jax.ShapeDtypeStruct((n, D_MODEL), F32),
        compiler_params=pltpu.CompilerParams(
            dimension_semantics=("arbitrary",), vmem_limit_bytes=VMEM_LIMIT),
        name="mixer_out",
    )(x, y, bonus, g, o_b, sr, sl, ln_w, ln_b, p_rwkv, p_lru, w_out, gn, hmean)


def _block_diag(w):
    nb, d, e = w.shape
    eye = jnp.eye(nb, dtype=w.dtype)
    return jnp.einsum("nde,nm->ndme", w, eye).reshape(nb * d, nb * e)


def kernel(x, norm_g, ffn1_w_up, ffn1_w_down, ffn2_w_up, ffn2_w_down, w_in, shift_mu, w0, w_up, a0, a_up, g_up, k_k, k_a, r_k, ln_w, ln_b, v0, v_down, v_up, conv_w, conv_b, lru_wa, lru_ba, lru_wx, lru_bx, lru_lambda, p_rwkv, p_lru, w_out):
    batch, seq, d = x.shape
    depth = norm_g.shape[0]
    assert d == D_MODEL and seq % MIX_TILE == 0 and (batch * seq) % FFN_TILE == 0
    xf = x.reshape(batch * seq, d)
    bf = lambda t: t.astype(BF16)
    row = lambda t: t.reshape(1, -1)
    head = jnp.arange(RWKV_WIDTH) // HEAD_DIM
    same = (head[:, None] == head[None, :])
    hsum = same.astype(BF16)
    hmean = (same.astype(F32) / HEAD_DIM).astype(BF16)
    zeros_lora = jnp.zeros((DECAY_LORA, RWKV_WIDTH), F32)
    v_first = None
    for l in range(depth):
        g = norm_g[l]
        xf = _ffn(xf, g[0:2], bf(ffn1_w_up[l]), bf(ffn1_w_down[l]))
        vmix = None if l == 0 else (row(v0[l - 1]), bf(v_down[l - 1]), bf(v_up[l - 1]), v_first)
        wup_pad = bf(jnp.concatenate([w_up[l], zeros_lora], axis=0))
        aup_pad = bf(jnp.concatenate([zeros_lora, a_up[l]], axis=0))
        wax = bf(jnp.concatenate([_block_diag(lru_wa[l]), _block_diag(lru_wx[l])], axis=1))
        bax = row(jnp.concatenate([lru_ba[l], lru_bx[l]]))
        r, lw, k, v, kk, a, bonus, gate, o_b, sr, sl = _mixer_in(
            xf, seq, row(g[2]), bf(w_in[l]), row(shift_mu[l]), row(w0[l]), wup_pad, row(a0[l]), aup_pad,
            bf(g_up[l]), row(k_k[l]), row(k_a[l]), row(r_k[l]), vmix,
            conv_w[l], row(conv_b[l]), wax, bax, row(lru_lambda[l]), hsum)
        if l == 0:
            v_first = v
        y = _rwkv(r, lw, k, v, kk, a, batch, seq)
        xf = _mixer_out(xf, y, bonus, gate, o_b, sr, sl, row(ln_w[l]), row(ln_b[l]),
                        bf(p_rwkv[l]), bf(p_lru[l]), bf(w_out[l]), row(g[3]), hmean)
        xf = _ffn(xf, g[4:6], bf(ffn2_w_up[l]), bf(ffn2_w_down[l]))
    return xf.reshape(batch, seq, d)
```

```python
import functools

import jax
import jax.numpy as jnp
from jax import lax
from jax.experimental import pallas as pl
from jax.experimental.pallas import tpu as pltpu

D_MODEL = 1024
RWKV_WIDTH = 512
HEAD_DIM = 64
LRU_WIDTH = 512
LRU_BLOCKS = 8
CONV_WIDTH = 4
LRU_C = 8.0
D_FF = 2816
RMS_EPS = 1e-6
GN_EPS = 64e-5
DECAY_LORA = 64
AAA_LORA = 64
GATE_LORA = 128
W_SHIFT = 3 * RWKV_WIDTH + DECAY_LORA + AAA_LORA + GATE_LORA
W_IN = W_SHIFT + 2 * LRU_WIDTH + 2 * D_MODEL

RWKV_CHUNKS = 2
CHUNK = 64
LANES = 128
SUBLANES = 8
FFN_TILE = 512
FFN_COLS = 256
MIX_TILE = 256
VMEM_LIMIT = 56 * 1024 * 1024

BF16 = jnp.bfloat16
F32 = jnp.float32


def _dot(a, b):
    return jnp.dot(a.astype(BF16), b.astype(BF16), preferred_element_type=F32)


def _dot_nt(a, b):
    return lax.dot_general(a.astype(BF16), b.astype(BF16), (((1,), (1,)), ((), ())),
                           preferred_element_type=F32)


def _dot_tn(a, b):
    return lax.dot_general(a.astype(BF16), b.astype(BF16), (((0,), (0,)), ((), ())),
                           preferred_element_type=F32)


def _dot_hilo(a, b_bf16):
    hi = a.astype(BF16)
    lo = (a - hi.astype(F32)).astype(BF16)
    return (jnp.dot(hi, b_bf16, preferred_element_type=F32)
            + jnp.dot(lo, b_bf16, preferred_element_type=F32))


def _rmsnorm(x, g):
    ms = jnp.mean(x * x, axis=-1, keepdims=True)
    return x * lax.rsqrt(ms + RMS_EPS) * g


def _softplus(x):
    return jnp.maximum(x, 0.0) + jnp.log1p(jnp.exp(-jnp.abs(x)))


def _shift_rows(prev_rows, x, d):
    ext = jnp.concatenate([prev_rows, x], axis=0)
    return pltpu.roll(ext, d, 0)[SUBLANES:]


def _full(shape):
    return pl.BlockSpec(shape, lambda *_: (0,) * len(shape))


def _ffn_kernel(x_ref, g_ref, wup_ref, wdn_ref, o_ref):
    x = x_ref[...]
    h = _rmsnorm(x, g_ref[0:1, :]).astype(BF16)
    acc = jnp.zeros(x.shape, F32)
    for j in range(D_FF // FFN_COLS):
        lo = j * FFN_COLS
        gate = jnp.dot(h, wup_ref[:, lo:lo + FFN_COLS], preferred_element_type=F32)
        up = jnp.dot(h, wup_ref[:, D_FF + lo:D_FF + lo + FFN_COLS], preferred_element_type=F32)
        act = (gate * jax.nn.sigmoid(gate) * up).astype(BF16)
        acc = acc + jnp.dot(act, wdn_ref[lo:lo + FFN_COLS, :], preferred_element_type=F32)
    o_ref[...] = x + 0.5 * _rmsnorm(acc, g_ref[1:2, :])


def _ffn(x, g2, w_up, w_down):
    n = x.shape[0]
    return pl.pallas_call(
        _ffn_kernel,
        grid=(n // FFN_TILE,),
        in_specs=[
            pl.BlockSpec((FFN_TILE, D_MODEL), lambda i: (i, 0)),
            _full((2, D_MODEL)),
            _full((D_MODEL, 2 * D_FF)),
            _full((D_FF, D_MODEL)),
        ],
        out_specs=pl.BlockSpec((FFN_TILE, D_MODEL), lambda i: (i, 0)),
        out_shape=jax.ShapeDtypeStruct((n, D_MODEL), F32),
        compiler_params=pltpu.CompilerParams(
            dimension_semantics=("arbitrary",), vmem_limit_bytes=VMEM_LIMIT),
        name="ffn",
    )(x, g2, w_up, w_down)


def _scan_rows(a, b):
    n = a.shape[0]
    row = lax.broadcasted_iota(jnp.int32, a.shape, 0)
    d = 1
    while d < n:
        keep = row >= d
        a_prev = jnp.where(keep, pltpu.roll(a, d, 0), 1.0)
        b_prev = jnp.where(keep, pltpu.roll(b, d, 0), 0.0)
        b = a * b_prev + b
        a = a * a_prev
        d *= 2
    return a, b


def _mixer_in_kernel(has_vmix, tiles_per_seq, *refs):
    (x_ref, g_ref, win_ref, mu_ref, w0_ref, wup_ref, a0_ref, aup_ref, gup_ref,
     kk_ref, ka_ref, rk_ref) = refs[:12]
    refs = refs[12:]
    if has_vmix:
        v0_ref, vdown_ref, vup_ref, vfirst_ref = refs[:4]
        refs = refs[4:]
    (convw_ref, convb_ref, wax_ref, bax_ref, lam_ref, hsum_ref,
     r_out, lw_out, k_out, v_out, kkn_out, a_out, bonus_out, g_out, ob_out, sr_out, sl_out,
     pcarry, xcarry, hcarry) = refs

    tm = x_ref.shape[0]
    first = (pl.program_id(0) % tiles_per_seq) == 0

    @pl.when(first)
    def _():
        pcarry[...] = jnp.zeros(pcarry.shape, F32)
        xcarry[...] = jnp.zeros(xcarry.shape, F32)
        hcarry[...] = jnp.zeros(hcarry.shape, F32)

    h = _rmsnorm(x_ref[...], g_ref[...])
    proj = _dot(h, win_ref[...])

    p = proj[:, :W_SHIFT]
    prev = _shift_rows(pcarry[...], p, 1)
    pcarry[...] = p[tm - SUBLANES:, :]
    p = p + mu_ref[...] * (prev - p)
    rw = RWKV_WIDTH
    r, k, v = p[:, 0:rw], p[:, rw:2 * rw], p[:, 2 * rw:3 * rw]
    xwa = p[:, 3 * rw:3 * rw + DECAY_LORA + AAA_LORA]
    xg = p[:, 3 * rw + DECAY_LORA + AAA_LORA:]
    w = -_softplus(-(w0_ref[...] + _dot(jnp.tanh(xwa), wup_ref[...]))) - 0.5
    lw_out[...] = -jnp.exp(w)
    a = jax.nn.sigmoid(a0_ref[...] + _dot(xwa, aup_ref[...]))
    g_out[...] = _dot(jax.nn.sigmoid(xg), gup_ref[...])
    if has_vmix:
        mix = jax.nn.sigmoid(v0_ref[...] + _dot(_dot(v, vdown_ref[...]), vup_ref[...]))
        v = v + (vfirst_ref[...] - v) * mix
    kk = k * kk_ref[...]
    norm = jnp.sqrt(_dot_hilo(kk * kk, hsum_ref[...]))
    kk = kk / jnp.maximum(norm, 1e-12)
    k = k * (1.0 + (a - 1.0) * ka_ref[...])
    r_out[...] = r
    k_out[...] = k
    v_out[...] = v
    kkn_out[...] = kk
    a_out[...] = a
    bonus_out[...] = _dot_hilo(r * k * rk_ref[...], hsum_ref[...]) * v

    lw_ = LRU_WIDTH
    lx = proj[:, W_SHIFT:W_SHIFT + lw_]
    lgate = proj[:, W_SHIFT + lw_:W_SHIFT + 2 * lw_]
    sr_out[...] = jax.nn.sigmoid(proj[:, W_SHIFT + 2 * lw_:W_SHIFT + 2 * lw_ + D_MODEL])
    sl_out[...] = jax.nn.sigmoid(proj[:, W_SHIFT + 2 * lw_ + D_MODEL:])
    xprev = xcarry[...]
    xb = convb_ref[...] + convw_ref[CONV_WIDTH - 1:CONV_WIDTH, :] * lx
    for d in range(1, CONV_WIDTH):
        xb = xb + convw_ref[CONV_WIDTH - 1 - d:CONV_WIDTH - d, :] * _shift_rows(xprev, lx, d)
    xcarry[...] = lx[tm - SUBLANES:, :]
    gates = _dot(xb, wax_ref[...]) + bax_ref[...]
    gate_a = jax.nn.sigmoid(gates[:, :lw_])
    gate_x = jax.nn.sigmoid(gates[:, lw_:])
    log_a = -LRU_C * gate_a * _softplus(-lam_ref[...])
    a_l = jnp.exp(log_a)
    mult = jnp.sqrt(-jnp.tanh(log_a) * (a_l * a_l + 1.0))
    row = lax.broadcasted_iota(jnp.int32, log_a.shape, 0)
    seq_start = row + jnp.where(first, 0, 1) == 0
    mult = jnp.where(seq_start, 1.0, mult)
    b_l = xb * gate_x * mult
    a_cum, b_cum = _scan_rows(a_l, b_l)
    hseq = a_cum * hcarry[SUBLANES - 1:SUBLANES, :] + b_cum
    hcarry[...] = hseq[tm - SUBLANES:, :]
    ob_out[...] = hseq * jax.nn.gelu(lgate)


def _mixer_in(x, seq, g, w_in, mu, w0, wup_pad, a0, aup_pad, g_up, k_k, k_a, r_k, vmix,
              conv_w, conv_b, wax, bax, lam, hsum):
    n = x.shape[0]
    tm = MIX_TILE
    has_vmix = vmix is not None
    row512 = pl.BlockSpec((tm, RWKV_WIDTH), lambda i: (i, 0))
    row1024 = pl.BlockSpec((tm, D_MODEL), lambda i: (i, 0))
    lora = DECAY_LORA + AAA_LORA
    in_specs = [
        row1024, _full((1, D_MODEL)), _full((D_MODEL, W_IN)), _full((1, W_SHIFT)),
        _full((1, RWKV_WIDTH)), _full((lora, RWKV_WIDTH)), _full((1, RWKV_WIDTH)),
        _full((lora, RWKV_WIDTH)), _full((GATE_LORA, RWKV_WIDTH)),
        _full((1, RWKV_WIDTH)), _full((1, RWKV_WIDTH)), _full((1, RWKV_WIDTH)),
    ]
    args = [x, g, w_in, mu, w0, wup_pad, a0, aup_pad, g_up, k_k, k_a, r_k]
    if has_vmix:
        v0, v_down, v_up, v_first = vmix
        in_specs += [_full((1, RWKV_WIDTH)), _full(v_down.shape), _full(v_up.shape), row512]
        args += [v0, v_down, v_up, v_first]
    in_specs += [
        _full((CONV_WIDTH, LRU_WIDTH)), _full((1, LRU_WIDTH)), _full((LRU_WIDTH, 2 * LRU_WIDTH)),
        _full((1, 2 * LRU_WIDTH)), _full((1, LRU_WIDTH)), _full((RWKV_WIDTH, RWKV_WIDTH)),
    ]
    args += [conv_w, conv_b, wax, bax, lam, hsum]
    out512 = jax.ShapeDtypeStruct((n, RWKV_WIDTH), F32)
    out1024 = jax.ShapeDtypeStruct((n, D_MODEL), F32)
    return pl.pallas_call(
        functools.partial(_mixer_in_kernel, has_vmix, seq // tm),
        grid=(n // tm,),
        in_specs=in_specs,
        out_specs=[row512] * 9 + [row1024] * 2,
        out_shape=[out512] * 9 + [out1024] * 2,
        scratch_shapes=[
            pltpu.VMEM((SUBLANES, W_SHIFT), F32),
            pltpu.VMEM((SUBLANES, LRU_WIDTH), F32),
            pltpu.VMEM((SUBLANES, LRU_WIDTH), F32),
        ],
        compiler_params=pltpu.CompilerParams(
            dimension_semantics=("arbitrary",), vmem_limit_bytes=VMEM_LIMIT),
        name="mixer_in",
    )(*args)


def _cumsum_rows(x):
    n = x.shape[0]
    row = lax.broadcasted_iota(jnp.int32, x.shape, 0)
    d = 1
    while d < n:
        x = x + jnp.where(row >= d, pltpu.roll(x, d, 0), 0.0)
        d *= 2
    return x


def _rwkv_kernel(r_ref, lw_ref, k_ref, v_ref, kk_ref, a_ref, y_ref, s_ref):
    c = CHUNK
    nb = r_ref.shape[0]
    pairs = RWKV_WIDTH // LANES

    @pl.when(pl.program_id(0) == 0)
    def _():
        s_ref[...] = jnp.zeros(s_ref.shape, F32)

    lane = lax.broadcasted_iota(jnp.int32, (c, LANES), 1)
    t_idx = lax.broadcasted_iota(jnp.int32, (c, LANES), 0)
    left = lane < HEAD_DIM
    s_idx = jnp.where(left, lane, lane - HEAD_DIM)
    strict = t_idx > s_idx
    incl = t_idx >= s_idx
    eye_w = (t_idx == s_idx).astype(F32)
    row2 = lax.broadcasted_iota(jnp.int32, (LANES, LANES), 0)
    lane2 = lax.broadcasted_iota(jnp.int32, (LANES, LANES), 1)
    same_head = (row2 < HEAD_DIM) == (lane2 < HEAD_DIM)
    eye2 = row2 == lane2

    def bd(w):
        return jnp.concatenate([jnp.where(left, w, 0.0), jnp.where(left, 0.0, w)], axis=0)

    ops = {}
    for bi in range(nb):
        for g in range(RWKV_CHUNKS):
            rows = slice(g * c, (g + 1) * c)
            lw = lw_ref[bi, rows, :]
            cum = _cumsum_rows(lw)
            last = cum[c - 1:c, :]
            kk = kk_ref[bi, rows, :]
            k = k_ref[bi, rows, :]
            b = kk * a_ref[bi, rows, :]
            e_neg = jnp.exp(-cum)
            e_tail = jnp.exp(last - cum)
            a_t = -kk * jnp.exp(cum - lw)
            r_t = r_ref[bi, rows, :] * jnp.exp(cum)
            b_t = b * e_neg
            k_t = k * e_neg
            b_h = b * e_tail
            k_h = k * e_tail
            w_c = jnp.exp(last)
            v = v_ref[bi, rows, :]
            for p in range(pairs):
                sl = slice(p * LANES, (p + 1) * LANES)
                w_col = jnp.sum(jnp.where(eye2, w_c[:, sl], 0.0), axis=1, keepdims=True)
                ops[bi, g, p] = dict(a=a_t[:, sl], r=r_t[:, sl], bt=b_t[:, sl], kt=k_t[:, sl],
                                     bh=b_h[:, sl], kh=k_h[:, sl], v=v[:, sl], w_col=w_col)
    chains = list(ops)

    for ch in chains:
        o = ops[ch]
        m = _dot_nt(jnp.concatenate([o["a"], o["r"]], axis=0),
                    jnp.concatenate([bd(o["bt"]), bd(o["kt"])], axis=0))
        o["a_ab"] = jnp.where(strict, m[:c, :LANES], 0.0)
        o["a_ak"] = jnp.where(strict, m[:c, LANES:], 0.0)
        o["a_rb"] = jnp.where(incl, m[c:, :LANES], 0.0)
        o["a_rk"] = jnp.where(incl, m[c:, LANES:], 0.0)
    for ch in chains:
        o = ops[ch]
        o["t"] = eye_w + o["a_ab"]
        o["p"] = _dot(o["a_ab"], bd(o["a_ab"]))
    steps = CHUNK.bit_length() - 2
    for i in range(steps):
        for ch in chains:
            o = ops[ch]
            p_bd = bd(o["p"])
            if i < steps - 1:
                prod = _dot(jnp.concatenate([o["t"], o["p"]], axis=0), p_bd)
                o["t"] = o["t"] + prod[:c]
                o["p"] = prod[c:]
            else:
                o["t"] = o["t"] + _dot(o["t"], p_bd)

    live = [(bi, p) for bi in range(nb) for p in range(pairs)]
    state = {key: s_ref[key[0], key[1]] for key in live}
    for g in range(RWKV_CHUNKS):
        rows = slice(g * c, (g + 1) * c)
        for bi, p in live:
            o = ops[bi, g, p]
            o["v_bd"] = bd(o["v"])
            o["x"] = _dot(jnp.concatenate([o["a"], o["a_ak"]], axis=1),
                          jnp.concatenate([state[bi, p], o["v_bd"]], axis=0))
        for bi, p in live:
            o = ops[bi, g, p]
            o["u"] = _dot(o["t"], bd(o["x"]))
        for bi, p in live:
            o = ops[bi, g, p]
            ds = _dot_tn(jnp.concatenate([o["bh"], o["kh"]], axis=0),
                         jnp.concatenate([o["u"], o["v"]], axis=0))
            o["s_new"] = o["w_col"] * state[bi, p] + jnp.where(same_head, ds, 0.0)
        for bi, p in live:
            o = ops[bi, g, p]
            y_ref[bi, rows, p * LANES:(p + 1) * LANES] = _dot(
                jnp.concatenate([o["r"], o["a_rb"], o["a_rk"]], axis=1),
                jnp.concatenate([state[bi, p], bd(o["u"]), o["v_bd"]], axis=0))
            state[bi, p] = o["s_new"]
    for (bi, p), s_val in state.items():
        s_ref[bi, p] = s_val


def _rwkv(r, lw, k, v, kk, a, batch, seq):
    rows = RWKV_CHUNKS * CHUNK
    spec = pl.BlockSpec((batch, rows, RWKV_WIDTH), lambda i: (0, i, 0))
    shape3 = (batch, seq, RWKV_WIDTH)
    y = pl.pallas_call(
        _rwkv_kernel,
        grid=(seq // rows,),
        in_specs=[spec] * 6,
        out_specs=spec,
        out_shape=jax.ShapeDtypeStruct(shape3, F32),
        scratch_shapes=[pltpu.VMEM((batch, RWKV_WIDTH // LANES, LANES, LANES), F32)],
        compiler_params=pltpu.CompilerParams(
            dimension_semantics=("arbitrary",), vmem_limit_bytes=VMEM_LIMIT),
        name="rwkv",
    )(*(t.reshape(shape3) for t in (r, lw, k, v, kk, a)))
    return y.reshape(batch * seq, RWKV_WIDTH)


def _mixer_out_kernel(x_ref, y_ref, bonus_ref, g_ref, ob_ref, sr_ref, sl_ref, lnw_ref, lnb_ref,
                      prw_ref, plr_ref, wout_ref, gn_ref, hmean_ref, o_ref):
    y = y_ref[...]
    mu = _dot_hilo(y, hmean_ref[...])
    yc = y - mu
    var = _dot_hilo(yc * yc, hmean_ref[...])
    yn = yc * lax.rsqrt(var + GN_EPS) * lnw_ref[...] + lnb_ref[...] + bonus_ref[...]
    o_a = yn * g_ref[...]
    merged = sr_ref[...] * _dot(o_a, prw_ref[...]) + sl_ref[...] * _dot(ob_ref[...], plr_ref[...])
    out = _dot(merged, wout_ref[...])
    o_ref[...] = x_ref[...] + _rmsnorm(out, gn_ref[...])


def _mixer_out(x, y, bonus, g, o_b, sr, sl, ln_w, ln_b, p_rwkv, p_lru, w_out, gn, hmean):
    n = x.shape[0]
    tm = MIX_TILE
    row512 = pl.BlockSpec((tm, RWKV_WIDTH), lambda i: (i, 0))
    row1024 = pl.BlockSpec((tm, D_MODEL), lambda i: (i, 0))
    return pl.pallas_call(
        _mixer_out_kernel,
        grid=(n // tm,),
        in_specs=[row1024, row512, row512, row512, row512, row1024, row1024,
                  _full((1, RWKV_WIDTH)), _full((1, RWKV_WIDTH)),
                  _full((RWKV_WIDTH, D_MODEL)), _full((LRU_WIDTH, D_MODEL)),
                  _full((D_MODEL, D_MODEL)), _full((1, D_MODEL)),
                  _full((RWKV_WIDTH, RWKV_WIDTH))],
        out_specs=row1024,
        out_shape=jax.ShapeDtypeStruct((n, D_MODEL), F32),
        compiler_params=pltpu.CompilerParams(
            dimension_semantics=("arbitrary",), vmem_limit_bytes=VMEM_LIMIT),
        name="mixer_out",
    )(x, y, bonus, g, o_b, sr, sl, ln_w, ln_b, p_rwkv, p_lru, w_out, gn, hmean)


def _block_diag(w):
    nb, d, e = w.shape
    eye = jnp.eye(nb, dtype=w.dtype)
    return jnp.einsum("nde,nm->ndme", w, eye).reshape(nb * d, nb * e)


def kernel(x, norm_g, ffn1_w_up, ffn1_w_down, ffn2_w_up, ffn2_w_down, w_in, shift_mu, w0, w_up, a0, a_up, g_up, k_k, k_a, r_k, ln_w, ln_b, v0, v_down, v_up, conv_w, conv_b, lru_wa, lru_ba, lru_wx, lru_bx, lru_lambda, p_rwkv, p_lru, w_out):
    batch, seq, d = x.shape
    depth = norm_g.shape[0]
    assert d == D_MODEL and seq % MIX_TILE == 0 and (batch * seq) % FFN_TILE == 0
    xf = x.reshape(batch * seq, d)
    bf = lambda t: t.astype(BF16)
    row = lambda t: t.reshape(1, -1)
    head = jnp.arange(RWKV_WIDTH) // HEAD_DIM
    same = (head[:, None] == head[None, :])
    hsum = same.astype(BF16)
    hmean = (same.astype(F32) / HEAD_DIM).astype(BF16)
    zeros_lora = jnp.zeros((DECAY_LORA, RWKV_WIDTH), F32)
    v_first = None
    for l in range(depth):
        g = norm_g[l]
        xf = _ffn(xf, g[0:2], bf(ffn1_w_up[l]), bf(ffn1_w_down[l]))
        vmix = None if l == 0 else (row(v0[l - 1]), bf(v_down[l - 1]), bf(v_up[l - 1]), v_first)
        wup_pad = bf(jnp.concatenate([w_up[l], zeros_lora], axis=0))
        aup_pad = bf(jnp.concatenate([zeros_lora, a_up[l]], axis=0))
        wax = bf(jnp.concatenate([_block_diag(lru_wa[l]), _block_diag(lru_wx[l])], axis=1))
        bax = row(jnp.concatenate([lru_ba[l], lru_bx[l]]))
        r, lw, k, v, kk, a, bonus, gate, o_b, sr, sl = _mixer_in(
            xf, seq, row(g[2]), bf(w_in[l]), row(shift_mu[l]), row(w0[l]), wup_pad, row(a0[l]), aup_pad,
            bf(g_up[l]), row(k_k[l]), row(k_a[l]), row(r_k[l]), vmix,
            conv_w[l], row(conv_b[l]), wax, bax, row(lru_lambda[l]), hsum)
        if l == 0:
            v_first = v
        y = _rwkv(r, lw, k, v, kk, a, batch, seq)
        xf = _mixer_out(xf, y, bonus, gate, o_b, sr, sl, row(ln_w[l]), row(ln_b[l]),
                        bf(p_rwkv[l]), bf(p_lru[l]), bf(w_out[l]), row(g[3]), hmean)
        xf = _ffn(xf, g[4:6], bf(ffn2_w_up[l]), bf(ffn2_w_down[l]))
    return xf.reshape(batch, seq, d)
```

```python
import functools

import jax
import jax.numpy as jnp
from jax import lax
from jax.experimental import pallas as pl
from jax.experimental.pallas import tpu as pltpu

D_MODEL = 1024
RWKV_WIDTH = 512
HEAD_DIM = 64
LRU_WIDTH = 512
LRU_BLOCKS = 8
CONV_WIDTH = 4
LRU_C = 8.0
D_FF = 2816
RMS_EPS = 1e-6
GN_EPS = 64e-5
DECAY_LORA = 64
AAA_LORA = 64
GATE_LORA = 128
W_SHIFT = 3 * RWKV_WIDTH + DECAY_LORA + AAA_LORA + GATE_LORA
W_MIX = W_SHIFT + 2 * LRU_WIDTH
W_IN = W_MIX + 2 * D_MODEL

RWKV_CHUNKS = 2
CHUNK = 64
LANES = 128
SUBLANES = 8
FFN_TILE = 512
FFN_COLS = 256
MIX_TILE = 256
PROJ_COLS = 256
VMEM_LIMIT = 56 * 1024 * 1024

BF16 = jnp.bfloat16
F32 = jnp.float32


def _dot(a, b):
    return jnp.dot(a.astype(BF16), b.astype(BF16), preferred_element_type=F32)


def _dot_nt(a, b):
    return lax.dot_general(a.astype(BF16), b.astype(BF16), (((1,), (1,)), ((), ())),
                           preferred_element_type=F32)


def _dot_tn(a, b):
    return lax.dot_general(a.astype(BF16), b.astype(BF16), (((0,), (0,)), ((), ())),
                           preferred_element_type=F32)


def _rmsnorm(x, g):
    ms = jnp.mean(x * x, axis=-1, keepdims=True)
    return x * lax.rsqrt(ms + RMS_EPS) * g


def _softplus(x, accurate):
    e = jnp.exp(-jnp.abs(x))
    return jnp.maximum(x, 0.0) + (jnp.log1p(e) if accurate else jnp.log(1.0 + e))


def _shift_rows(prev_rows, x, d):
    ext = jnp.concatenate([prev_rows, x], axis=0)
    return pltpu.roll(ext, d, 0)[SUBLANES:]


def _full(shape):
    return pl.BlockSpec(shape, lambda *_: (0,) * len(shape))


def _ffn_kernel(x_ref, g_ref, wup_ref, wdn_ref, o_ref):
    x = x_ref[...]
    h = _rmsnorm(x, g_ref[0:1, :]).astype(BF16)
    acc = jnp.zeros(x.shape, F32)
    for j in range(D_FF // FFN_COLS):
        lo = j * FFN_COLS
        gate = jnp.dot(h, wup_ref[:, lo:lo + FFN_COLS], preferred_element_type=F32)
        up = jnp.dot(h, wup_ref[:, D_FF + lo:D_FF + lo + FFN_COLS], preferred_element_type=F32)
        act = (gate * jax.nn.sigmoid(gate) * up).astype(BF16)
        acc = acc + jnp.dot(act, wdn_ref[lo:lo + FFN_COLS, :], preferred_element_type=F32)
    o_ref[...] = x + 0.5 * _rmsnorm(acc, g_ref[1:2, :])


def _ffn(x, g2, w_up, w_down):
    n = x.shape[0]
    return pl.pallas_call(
        _ffn_kernel,
        grid=(n // FFN_TILE,),
        in_specs=[
            pl.BlockSpec((FFN_TILE, D_MODEL), lambda i: (i, 0)),
            _full((2, D_MODEL)),
            _full((D_MODEL, 2 * D_FF)),
            _full((D_FF, D_MODEL)),
        ],
        out_specs=pl.BlockSpec((FFN_TILE, D_MODEL), lambda i: (i, 0)),
        out_shape=jax.ShapeDtypeStruct((n, D_MODEL), F32),
        compiler_params=pltpu.CompilerParams(
            dimension_semantics=("arbitrary",), vmem_limit_bytes=VMEM_LIMIT),
        name="ffn",
    )(x, g2, w_up, w_down)


def _scan_rows(a, b, h0):
    n, w = a.shape
    groups = n // SUBLANES
    a = a.reshape(groups, SUBLANES, w)
    b = b.reshape(groups, SUBLANES, w)
    sub = lax.broadcasted_iota(jnp.int32, a.shape, 1)
    d = 1
    while d < SUBLANES:
        keep = sub >= d
        a_prev = jnp.where(keep, pltpu.roll(a, d, 1), 1.0)
        b_prev = jnp.where(keep, pltpu.roll(b, d, 1), 0.0)
        b = a * b_prev + b
        a = a * a_prev
        d *= 2
    out = []
    for i in range(groups):
        hi = a[i] * h0 + b[i]
        out.append(hi)
        h0 = hi[SUBLANES - 1:SUBLANES, :]
    return jnp.concatenate(out, axis=0)


def _mixer_in_kernel(has_vmix, tiles_per_seq, *refs):
    (x_ref, g_ref, win_ref, mu_ref, w0_ref, wup_ref, a0_ref, aup_ref, gup_ref,
     kk_ref, ka_ref, rk_ref) = refs[:12]
    refs = refs[12:]
    if has_vmix:
        v0_ref, vdown_ref, vup_ref, vfirst_ref = refs[:4]
        refs = refs[4:]
    (convw_ref, convb_ref, wax_ref, bax_ref, lam_ref, hsum_ref,
     r_out, lw_out, k_out, v_out, kkn_out, a_out, bonus_out, g_out, ob_out,
     proj_even, proj_odd, pcarry, xcarry, hcarry) = refs

    tm = x_ref.shape[0]
    step = pl.program_id(0)
    first = (step + tiles_per_seq - 1) % tiles_per_seq == 0

    @pl.when(step == 0)
    def _():
        proj_odd[...] = jnp.zeros(proj_odd.shape, F32)

    @pl.when(jnp.logical_or(first, step == 0))
    def _():
        pcarry[...] = jnp.zeros(pcarry.shape, F32)
        xcarry[...] = jnp.zeros(xcarry.shape, F32)
        hcarry[...] = jnp.zeros(hcarry.shape, F32)

    def body(proj_new, proj):
        h = _rmsnorm(x_ref[...], g_ref[...]).astype(BF16)
        chunk_starts = iter(range(0, W_MIX, PROJ_COLS))

        def project(n_chunks):
            for _ in range(n_chunks):
                lo = next(chunk_starts)
                proj_new[:, lo:lo + PROJ_COLS] = jnp.dot(
                    h, win_ref[:, lo:lo + PROJ_COLS], preferred_element_type=F32)

        project(2)
        p = proj[:, :W_SHIFT]
        prev = _shift_rows(pcarry[...], p, 1)
        pcarry[...] = p[tm - SUBLANES:, :]
        p = p + mu_ref[...] * (prev - p)
        rw = RWKV_WIDTH
        r, k, v = p[:, 0:rw], p[:, rw:2 * rw], p[:, 2 * rw:3 * rw]
        xwa = p[:, 3 * rw:3 * rw + DECAY_LORA + AAA_LORA]
        xg = p[:, 3 * rw + DECAY_LORA + AAA_LORA:]
        w = -_softplus(-(w0_ref[...] + _dot(jnp.tanh(xwa), wup_ref[...])), accurate=False) - 0.5
        a = jax.nn.sigmoid(a0_ref[...] + _dot(xwa, aup_ref[...]))
        g_out[...] = _dot(jax.nn.sigmoid(xg), gup_ref[...])
        project(2)
        lw_out[...] = -jnp.exp(w)
        if has_vmix:
            mix = jax.nn.sigmoid(v0_ref[...] + _dot(_dot(v, vdown_ref[...]), vup_ref[...]))
            v = v + (vfirst_ref[...] - v) * mix
        kk = k * kk_ref[...]
        kk = kk * lax.rsqrt(jnp.maximum(_dot(kk * kk, hsum_ref[...]), 1e-24))
        project(2)
        k = k * (1.0 + (a - 1.0) * ka_ref[...])
        r_out[...] = r
        k_out[...] = k
        v_out[...] = v
        kkn_out[...] = kk
        a_out[...] = a
        bonus_out[...] = _dot(r * k * rk_ref[...], hsum_ref[...]) * v
        project(2)

        lw_ = LRU_WIDTH
        lx = proj[:, W_SHIFT:W_SHIFT + lw_]
        lgate = proj[:, W_SHIFT + lw_:]
        xprev = xcarry[...]
        xb = convb_ref[...] + convw_ref[CONV_WIDTH - 1:CONV_WIDTH, :] * lx
        for d in range(1, CONV_WIDTH):
            xb = xb + convw_ref[CONV_WIDTH - 1 - d:CONV_WIDTH - d, :] * _shift_rows(xprev, lx, d)
        xcarry[...] = lx[tm - SUBLANES:, :]
        gates = _dot(xb, wax_ref[...]) + bax_ref[...]
        project(W_MIX // PROJ_COLS - 8)
        gate_a = jax.nn.sigmoid(gates[:, :lw_])
        gate_x = jax.nn.sigmoid(gates[:, lw_:])
        log_a = -LRU_C * gate_a * _softplus(-lam_ref[...], accurate=True)
        a_l = jnp.exp(log_a)
        mult = jnp.sqrt(-jnp.tanh(log_a) * (a_l * a_l + 1.0))
        row = lax.broadcasted_iota(jnp.int32, log_a.shape, 0)
        seq_start = row + jnp.where(first, 0, 1) == 0
        mult = jnp.where(seq_start, 1.0, mult)
        hseq = _scan_rows(a_l, xb * gate_x * mult, hcarry[SUBLANES - 1:SUBLANES, :])
        hcarry[...] = hseq[tm - SUBLANES:, :]
        ob_out[...] = hseq * jax.nn.gelu(lgate)

    @pl.when(step % 2 == 0)
    def _():
        body(proj_even, proj_odd)

    @pl.when(step % 2 == 1)
    def _():
        body(proj_odd, proj_even)


def _mixer_in(x, seq, g, w_mix, mu, w0, wup_pad, a0, aup_pad, g_up, k_k, k_a, r_k, vmix,
              conv_w, conv_b, wax, bax, lam, hsum):
    n = x.shape[0]
    tm = MIX_TILE
    tiles = n // tm
    has_vmix = vmix is not None
    x_spec = pl.BlockSpec((tm, D_MODEL), lambda i: (jnp.minimum(i, tiles - 1), 0))
    lag512 = pl.BlockSpec((tm, RWKV_WIDTH), lambda i: (jnp.maximum(i - 1, 0), 0))
    lora = DECAY_LORA + AAA_LORA
    in_specs = [
        x_spec, _full((1, D_MODEL)), _full((D_MODEL, W_MIX)), _full((1, W_SHIFT)),
        _full((1, RWKV_WIDTH)), _full((lora, RWKV_WIDTH)), _full((1, RWKV_WIDTH)),
        _full((lora, RWKV_WIDTH)), _full((GATE_LORA, RWKV_WIDTH)),
        _full((1, RWKV_WIDTH)), _full((1, RWKV_WIDTH)), _full((1, RWKV_WIDTH)),
    ]
    args = [x, g, w_mix, mu, w0, wup_pad, a0, aup_pad, g_up, k_k, k_a, r_k]
    if has_vmix:
        v0, v_down, v_up, v_first = vmix
        in_specs += [_full((1, RWKV_WIDTH)), _full(v_down.shape), _full(v_up.shape), lag512]
        args += [v0, v_down, v_up, v_first]
    in_specs += [
        _full((CONV_WIDTH, LRU_WIDTH)), _full((1, LRU_WIDTH)), _full((LRU_WIDTH, 2 * LRU_WIDTH)),
        _full((1, 2 * LRU_WIDTH)), _full((1, LRU_WIDTH)), _full((RWKV_WIDTH, RWKV_WIDTH)),
    ]
    args += [conv_w, conv_b, wax, bax, lam, hsum]
    out512 = jax.ShapeDtypeStruct((n, RWKV_WIDTH), F32)
    return pl.pallas_call(
        functools.partial(_mixer_in_kernel, has_vmix, seq // tm),
        grid=(tiles + 1,),
        in_specs=in_specs,
        out_specs=[lag512] * 9,
        out_shape=[out512] * 9,
        scratch_shapes=[
            pltpu.VMEM((tm, W_MIX), F32),
            pltpu.VMEM((tm, W_MIX), F32),
            pltpu.VMEM((SUBLANES, W_SHIFT), F32),
            pltpu.VMEM((SUBLANES, LRU_WIDTH), F32),
            pltpu.VMEM((SUBLANES, LRU_WIDTH), F32),
        ],
        compiler_params=pltpu.CompilerParams(
            dimension_semantics=("arbitrary",), vmem_limit_bytes=VMEM_LIMIT),
        name="mixer_in",
    )(*args)


def _cumsum_rows(x):
    n = x.shape[0]
    row = lax.broadcasted_iota(jnp.int32, x.shape, 0)
    d = 1
    while d < n:
        x = x + jnp.where(row >= d, pltpu.roll(x, d, 0), 0.0)
        d *= 2
    return x


def _rwkv_kernel(r_ref, lw_ref, k_ref, v_ref, kk_ref, a_ref, y_ref, s_ref):
    c = CHUNK
    nb = r_ref.shape[0]
    pairs = RWKV_WIDTH // LANES

    @pl.when(pl.program_id(0) == 0)
    def _():
        s_ref[...] = jnp.zeros(s_ref.shape, F32)

    lane = lax.broadcasted_iota(jnp.int32, (c, LANES), 1)
    t_idx = lax.broadcasted_iota(jnp.int32, (c, LANES), 0)
    left = lane < HEAD_DIM
    s_idx = jnp.where(left, lane, lane - HEAD_DIM)
    strict = t_idx > s_idx
    incl = t_idx >= s_idx
    eye_w = (t_idx == s_idx).astype(F32)
    row2 = lax.broadcasted_iota(jnp.int32, (LANES, LANES), 0)
    lane2 = lax.broadcasted_iota(jnp.int32, (LANES, LANES), 1)
    same_head = (row2 < HEAD_DIM) == (lane2 < HEAD_DIM)
    eye2 = row2 == lane2

    def bd(w):
        return jnp.concatenate([jnp.where(left, w, 0.0), jnp.where(left, 0.0, w)], axis=0)

    ops = {}
    for bi in range(nb):
        for g in range(RWKV_CHUNKS):
            rows = slice(g * c, (g + 1) * c)
            lw = lw_ref[bi, rows, :]
            cum = _cumsum_rows(lw)
            last = cum[c - 1:c, :]
            kk = kk_ref[bi, rows, :]
            k = k_ref[bi, rows, :]
            b = kk * a_ref[bi, rows, :]
            e_neg = jnp.exp(-cum)
            e_tail = jnp.exp(last - cum)
            a_t = -kk * jnp.exp(cum - lw)
            r_t = r_ref[bi, rows, :] * jnp.exp(cum)
            b_t = b * e_neg
            k_t = k * e_neg
            b_h = b * e_tail
            k_h = k * e_tail
            w_c = jnp.exp(last)
            v = v_ref[bi, rows, :]
            for p in range(pairs):
                sl = slice(p * LANES, (p + 1) * LANES)
                w_col = jnp.sum(jnp.where(eye2, w_c[:, sl], 0.0), axis=1, keepdims=True)
                ops[bi, g, p] = dict(a=a_t[:, sl], r=r_t[:, sl], bt=b_t[:, sl], kt=k_t[:, sl],
                                     bh=b_h[:, sl], kh=k_h[:, sl], v=v[:, sl], w_col=w_col)
    chains = list(ops)

    for ch in chains:
        o = ops[ch]
        m = _dot_nt(jnp.concatenate([o["a"], o["r"]], axis=0),
                    jnp.concatenate([bd(o["bt"]), bd(o["kt"])], axis=0))
        o["a_ab"] = jnp.where(strict, m[:c, :LANES], 0.0)
        o["a_ak"] = jnp.where(strict, m[:c, LANES:], 0.0)
        o["a_rb"] = jnp.where(incl, m[c:, :LANES], 0.0)
        o["a_rk"] = jnp.where(incl, m[c:, LANES:], 0.0)
    for ch in chains:
        o = ops[ch]
        o["t"] = eye_w + o["a_ab"]
        o["p"] = _dot(o["a_ab"], bd(o["a_ab"]))
    steps = CHUNK.bit_length() - 2
    for i in range(steps):
        for ch in chains:
            o = ops[ch]
            p_bd = bd(o["p"])
            if i < steps - 1:
                prod = _dot(jnp.concatenate([o["t"], o["p"]], axis=0), p_bd)
                o["t"] = o["t"] + prod[:c]
                o["p"] = prod[c:]
            else:
                o["t"] = o["t"] + _dot(o["t"], p_bd)

    live = [(bi, p) for bi in range(nb) for p in range(pairs)]
    state = {key: s_ref[key[0], key[1]] for key in live}
    for g in range(RWKV_CHUNKS):
        rows = slice(g * c, (g + 1) * c)
        for bi, p in live:
            o = ops[bi, g, p]
            o["v_bd"] = bd(o["v"])
            o["x"] = _dot(jnp.concatenate([o["a"], o["a_ak"]], axis=1),
                          jnp.concatenate([state[bi, p], o["v_bd"]], axis=0))
        for bi, p in live:
            o = ops[bi, g, p]
            o["u"] = _dot(o["t"], bd(o["x"]))
        for bi, p in live:
            o = ops[bi, g, p]
            ds = _dot_tn(jnp.concatenate([o["bh"], o["kh"]], axis=0),
                         jnp.concatenate([o["u"], o["v"]], axis=0))
            o["s_new"] = o["w_col"] * state[bi, p] + jnp.where(same_head, ds, 0.0)
        for bi, p in live:
            o = ops[bi, g, p]
            y_ref[bi, rows, p * LANES:(p + 1) * LANES] = _dot(
                jnp.concatenate([o["r"], o["a_rb"], o["a_rk"]], axis=1),
                jnp.concatenate([state[bi, p], bd(o["u"]), o["v_bd"]], axis=0))
            state[bi, p] = o["s_new"]
    for (bi, p), s_val in state.items():
        s_ref[bi, p] = s_val


def _rwkv(r, lw, k, v, kk, a, batch, seq):
    rows = RWKV_CHUNKS * CHUNK
    spec = pl.BlockSpec((batch, rows, RWKV_WIDTH), lambda i: (0, i, 0))
    shape3 = (batch, seq, RWKV_WIDTH)
    y = pl.pallas_call(
        _rwkv_kernel,
        grid=(seq // rows,),
        in_specs=[spec] * 6,
        out_specs=spec,
        out_shape=jax.ShapeDtypeStruct(shape3, F32),
        scratch_shapes=[pltpu.VMEM((batch, RWKV_WIDTH // LANES, LANES, LANES), F32)],
        compiler_params=pltpu.CompilerParams(
            dimension_semantics=("arbitrary",), vmem_limit_bytes=VMEM_LIMIT),
        name="rwkv",
    )(*(t.reshape(shape3) for t in (r, lw, k, v, kk, a)))
    return y.reshape(batch * seq, RWKV_WIDTH)


def _mixer_out_kernel(x_ref, y_ref, bonus_ref, g_ref, ob_ref, gpre_ref, wgate_ref, lnw_ref, lnb_ref,
                      prw_ref, plr_ref, wout_ref, gn_ref, hmean_ref, o_ref):
    x = x_ref[...]
    gates = jax.nn.sigmoid(_dot(_rmsnorm(x, gpre_ref[...]), wgate_ref[...]))
    y = y_ref[...]
    mu = _dot(y, hmean_ref[...])
    yc = y - mu
    var = _dot(yc * yc, hmean_ref[...])
    yn = yc * lax.rsqrt(var + GN_EPS) * lnw_ref[...] + lnb_ref[...] + bonus_ref[...]
    o_a = yn * g_ref[...]
    merged = (gates[:, :D_MODEL] * _dot(o_a, prw_ref[...])
              + gates[:, D_MODEL:] * _dot(ob_ref[...], plr_ref[...]))
    out = _dot(merged, wout_ref[...])
    o_ref[...] = x + _rmsnorm(out, gn_ref[...])


def _mixer_out(x, y, bonus, g, o_b, g_pre, w_gate, ln_w, ln_b, p_rwkv, p_lru, w_out, gn, hmean):
    n = x.shape[0]
    tm = MIX_TILE
    row512 = pl.BlockSpec((tm, RWKV_WIDTH), lambda i: (i, 0))
    row1024 = pl.BlockSpec((tm, D_MODEL), lambda i: (i, 0))
    return pl.pallas_call(
        _mixer_out_kernel,
        grid=(n // tm,),
        in_specs=[row1024, row512, row512, row512, row512,
                  _full((1, D_MODEL)), _full((D_MODEL, 2 * D_MODEL)),
                  _full((1, RWKV_WIDTH)), _full((1, RWKV_WIDTH)),
                  _full((RWKV_WIDTH, D_MODEL)), _full((LRU_WIDTH, D_MODEL)),
                  _full((D_MODEL, D_MODEL)), _full((1, D_MODEL)),
                  _full((RWKV_WIDTH, RWKV_WIDTH))],
        out_specs=row1024,
        out_shape=jax.ShapeDtypeStruct((n, D_MODEL), F32),
        compiler_params=pltpu.CompilerParams(
            dimension_semantics=("arbitrary",), vmem_limit_bytes=VMEM_LIMIT),
        name="mixer_out",
    )(x, y, bonus, g, o_b, g_pre, w_gate, ln_w, ln_b, p_rwkv, p_lru, w_out, gn, hmean)


def _block_diag(w):
    nb, d, e = w.shape
    eye = jnp.eye(nb, dtype=w.dtype)
    return jnp.einsum("nde,nm->ndme", w, eye).reshape(nb * d, nb * e)


def kernel(x, norm_g, ffn1_w_up, ffn1_w_down, ffn2_w_up, ffn2_w_down, w_in, shift_mu, w0, w_up, a0, a_up, g_up, k_k, k_a, r_k, ln_w, ln_b, v0, v_down, v_up, conv_w, conv_b, lru_wa, lru_ba, lru_wx, lru_bx, lru_lambda, p_rwkv, p_lru, w_out):
    batch, seq, d = x.shape
    depth = norm_g.shape[0]
    assert d == D_MODEL and seq % MIX_TILE == 0 and (batch * seq) % FFN_TILE == 0
    assert seq % (RWKV_CHUNKS * CHUNK) == 0
    xf = x.reshape(batch * seq, d)
    bf = lambda t: t.astype(BF16)
    row = lambda t: t.reshape(1, -1)
    head = jnp.arange(RWKV_WIDTH) // HEAD_DIM
    same = (head[:, None] == head[None, :])
    hsum = same.astype(BF16)
    hmean = (same.astype(F32) / HEAD_DIM).astype(BF16)
    zeros_lora = jnp.zeros((DECAY_LORA, RWKV_WIDTH), F32)
    v_first = None
    for l in range(depth):
        g = norm_g[l]
        xf = _ffn(xf, g[0:2], bf(ffn1_w_up[l]), bf(ffn1_w_down[l]))
        vmix = None if l == 0 else (row(v0[l - 1]), bf(v_down[l - 1]), bf(v_up[l - 1]), v_first)
        wup_pad = bf(jnp.concatenate([w_up[l], zeros_lora], axis=0))
        aup_pad = bf(jnp.concatenate([zeros_lora, a_up[l]], axis=0))
        wax = bf(jnp.concatenate([_block_diag(lru_wa[l]), _block_diag(lru_wx[l])], axis=1))
        bax = row(jnp.concatenate([lru_ba[l], lru_bx[l]]))
        w_in_l = bf(w_in[l])
        r, lw, k, v, kk, a, bonus, gate, o_b = _mixer_in(
            xf, seq, row(g[2]), w_in_l[:, :W_MIX], row(shift_mu[l]), row(w0[l]), wup_pad, row(a0[l]),
            aup_pad, bf(g_up[l]), row(k_k[l]), row(k_a[l]), row(r_k[l]), vmix,
            conv_w[l], row(conv_b[l]), wax, bax, row(lru_lambda[l]), hsum)
        if l == 0:
            v_first = v
        y = _rwkv(r, lw, k, v, kk, a, batch, seq)
        xf = _mixer_out(xf, y, bonus, gate, o_b, row(g[2]), w_in_l[:, W_MIX:], row(ln_w[l]), row(ln_b[l]),
                        bf(p_rwkv[l]), bf(p_lru[l]), bf(w_out[l]), row(g[3]), hmean)
        xf = _ffn(xf, g[4:6], bf(ffn2_w_up[l]), bf(ffn2_w_down[l]))
    return xf.reshape(batch, seq, d)
```

```python
import functools

import jax
import jax.numpy as jnp
from jax import lax
from jax.experimental import pallas as pl
from jax.experimental.pallas import tpu as pltpu

D_MODEL = 1024
RWKV_WIDTH = 512
HEAD_DIM = 64
LRU_WIDTH = 512
LRU_BLOCKS = 8
CONV_WIDTH = 4
LRU_C = 8.0
D_FF = 2816
RMS_EPS = 1e-6
GN_EPS = 64e-5
DECAY_LORA = 64
AAA_LORA = 64
GATE_LORA = 128
W_SHIFT = 3 * RWKV_WIDTH + DECAY_LORA + AAA_LORA + GATE_LORA
W_MIX = W_SHIFT + 2 * LRU_WIDTH
W_IN = W_MIX + 2 * D_MODEL

RWKV_CHUNKS = 8
CHUNK = 64
LANES = 128
SUBLANES = 8
FFN_TILE = 512
FFN_COLS = 256
MIX_TILE = 256
PROJ_COLS = 256
PROJ_SPLIT = (2, 2, 2, 2, 3)
VMEM_LIMIT = 56 * 1024 * 1024

BF16 = jnp.bfloat16
F32 = jnp.float32


def _dot(a, b):
    return jnp.dot(a.astype(BF16), b.astype(BF16), preferred_element_type=F32)


def _dot_nt(a, b):
    return lax.dot_general(a.astype(BF16), b.astype(BF16), (((1,), (1,)), ((), ())),
                           preferred_element_type=F32)


def _dot_tn(a, b):
    return lax.dot_general(a.astype(BF16), b.astype(BF16), (((0,), (0,)), ((), ())),
                           preferred_element_type=F32)


def _rmsnorm(x, g):
    ms = jnp.mean(x * x, axis=-1, keepdims=True)
    return x * lax.rsqrt(ms + RMS_EPS) * g


def _softplus(x, accurate):
    e = jnp.exp(-jnp.abs(x))
    return jnp.maximum(x, 0.0) + (jnp.log1p(e) if accurate else jnp.log(1.0 + e))


def _shift_rows(prev_rows, x, d):
    ext = jnp.concatenate([prev_rows, x], axis=0)
    return pltpu.roll(ext, d, 0)[SUBLANES:]


def _full(shape):
    return pl.BlockSpec(shape, lambda *_: (0,) * len(shape))


def _ffn_kernel(x_ref, g_ref, wup_ref, wdn_ref, o_ref):
    x = x_ref[...]
    h = _rmsnorm(x, g_ref[0:1, :]).astype(BF16)
    acc = jnp.zeros(x.shape, F32)
    for j in range(D_FF // FFN_COLS):
        lo = j * FFN_COLS
        gate = jnp.dot(h, wup_ref[:, lo:lo + FFN_COLS], preferred_element_type=F32)
        up = jnp.dot(h, wup_ref[:, D_FF + lo:D_FF + lo + FFN_COLS], preferred_element_type=F32)
        act = (gate * jax.nn.sigmoid(gate) * up).astype(BF16)
        acc = acc + jnp.dot(act, wdn_ref[lo:lo + FFN_COLS, :], preferred_element_type=F32)
    o_ref[...] = x + 0.5 * _rmsnorm(acc, g_ref[1:2, :])


def _ffn(x, g2, w_up, w_down):
    n = x.shape[0]
    return pl.pallas_call(
        _ffn_kernel,
        grid=(n // FFN_TILE,),
        in_specs=[
            pl.BlockSpec((FFN_TILE, D_MODEL), lambda i: (i, 0)),
            _full((2, D_MODEL)),
            _full((D_MODEL, 2 * D_FF)),
            _full((D_FF, D_MODEL)),
        ],
        out_specs=pl.BlockSpec((FFN_TILE, D_MODEL), lambda i: (i, 0)),
        out_shape=jax.ShapeDtypeStruct((n, D_MODEL), F32),
        compiler_params=pltpu.CompilerParams(
            dimension_semantics=("arbitrary",), vmem_limit_bytes=VMEM_LIMIT),
        name="ffn",
    )(x, g2, w_up, w_down)


def _scan_rows(a, b, h0):
    n, w = a.shape
    groups = n // SUBLANES
    a = a.reshape(groups, SUBLANES, w)
    b = b.reshape(groups, SUBLANES, w)
    sub = lax.broadcasted_iota(jnp.int32, a.shape, 1)
    d = 1
    while d < SUBLANES:
        keep = sub >= d
        a_prev = jnp.where(keep, pltpu.roll(a, d, 1), 1.0)
        b_prev = jnp.where(keep, pltpu.roll(b, d, 1), 0.0)
        b = a * b_prev + b
        a = a * a_prev
        d *= 2
    out = []
    for i in range(groups):
        hi = a[i] * h0 + b[i]
        out.append(hi)
        h0 = hi[SUBLANES - 1:SUBLANES, :]
    return jnp.concatenate(out, axis=0)


def _mixer_in_kernel(has_vmix, tiles_per_seq, *refs):
    (x_ref, g_ref, win_ref, mu_ref, w0_ref, wup_ref, a0_ref, aup_ref, gup_ref,
     kk_ref, ka_ref, rk_ref) = refs[:12]
    refs = refs[12:]
    if has_vmix:
        v0_ref, vdown_ref, vup_ref, vfirst_ref = refs[:4]
        refs = refs[4:]
    (convw_ref, convb_ref, wax_ref, bax_ref, lam_ref, hsum_ref,
     r_out, lw_out, k_out, v_out, kkn_out, a_out, bonus_out, g_out, ob_out,
     proj_even, proj_odd, pcarry, xcarry, hcarry) = refs

    tm = x_ref.shape[0]
    step = pl.program_id(0)
    first = (step + tiles_per_seq - 1) % tiles_per_seq == 0

    @pl.when(step == 0)
    def _():
        proj_odd[...] = jnp.zeros(proj_odd.shape, F32)

    @pl.when(jnp.logical_or(first, step == 0))
    def _():
        pcarry[...] = jnp.zeros(pcarry.shape, F32)
        xcarry[...] = jnp.zeros(xcarry.shape, F32)
        hcarry[...] = jnp.zeros(hcarry.shape, F32)

    def body(proj_new, proj):
        h = _rmsnorm(x_ref[...], g_ref[...]).astype(BF16)
        chunk_starts = iter(range(0, W_MIX, PROJ_COLS))

        def project(n_chunks):
            for _ in range(n_chunks):
                lo = next(chunk_starts)
                proj_new[:, lo:lo + PROJ_COLS] = jnp.dot(
                    h, win_ref[:, lo:lo + PROJ_COLS], preferred_element_type=F32)

        project(PROJ_SPLIT[0])
        p = proj[:, :W_SHIFT]
        prev = _shift_rows(pcarry[...], p, 1)
        pcarry[...] = p[tm - SUBLANES:, :]
        p = p + mu_ref[...] * (prev - p)
        rw = RWKV_WIDTH
        r, k, v = p[:, 0:rw], p[:, rw:2 * rw], p[:, 2 * rw:3 * rw]
        xwa = p[:, 3 * rw:3 * rw + DECAY_LORA + AAA_LORA]
        xg = p[:, 3 * rw + DECAY_LORA + AAA_LORA:]
        w = -_softplus(-(w0_ref[...] + _dot(jnp.tanh(xwa), wup_ref[...])), accurate=False) - 0.5
        a = jax.nn.sigmoid(a0_ref[...] + _dot(xwa, aup_ref[...]))
        g_out[...] = _dot(jax.nn.sigmoid(xg), gup_ref[...])
        if has_vmix:
            v_lora = _dot(v, vdown_ref[...])
        project(PROJ_SPLIT[1])
        lw_out[...] = -jnp.exp(w)
        if has_vmix:
            mix = jax.nn.sigmoid(v0_ref[...] + _dot(v_lora, vup_ref[...]))
            v = v + (vfirst_ref[...] - v) * mix
        kk = k * kk_ref[...]
        kk = kk * lax.rsqrt(jnp.maximum(_dot(kk * kk, hsum_ref[...]), 1e-24))
        project(PROJ_SPLIT[2])
        k = k * (1.0 + (a - 1.0) * ka_ref[...])
        r_out[...] = r
        k_out[...] = k
        v_out[...] = v
        kkn_out[...] = kk
        a_out[...] = a
        bonus_out[...] = _dot(r * k * rk_ref[...], hsum_ref[...]) * v
        project(PROJ_SPLIT[3])

        lw_ = LRU_WIDTH
        lx = proj[:, W_SHIFT:W_SHIFT + lw_]
        lgate = proj[:, W_SHIFT + lw_:]
        xprev = xcarry[...]
        xb = convb_ref[...] + convw_ref[CONV_WIDTH - 1:CONV_WIDTH, :] * lx
        for d in range(1, CONV_WIDTH):
            xb = xb + convw_ref[CONV_WIDTH - 1 - d:CONV_WIDTH - d, :] * _shift_rows(xprev, lx, d)
        xcarry[...] = lx[tm - SUBLANES:, :]
        gates = _dot(xb, wax_ref[...]) + bax_ref[...]
        project(PROJ_SPLIT[4])
        gate_a = jax.nn.sigmoid(gates[:, :lw_])
        gate_x = jax.nn.sigmoid(gates[:, lw_:])
        log_a = -LRU_C * gate_a * _softplus(-lam_ref[...], accurate=True)
        a_l = jnp.exp(log_a)
        mult = jnp.sqrt(-jnp.tanh(log_a) * (a_l * a_l + 1.0))
        row = lax.broadcasted_iota(jnp.int32, log_a.shape, 0)
        seq_start = row + jnp.where(first, 0, 1) == 0
        mult = jnp.where(seq_start, 1.0, mult)
        hseq = _scan_rows(a_l, xb * gate_x * mult, hcarry[SUBLANES - 1:SUBLANES, :])
        hcarry[...] = hseq[tm - SUBLANES:, :]
        ob_out[...] = hseq * jax.nn.gelu(lgate)

    @pl.when(step % 2 == 0)
    def _():
        body(proj_even, proj_odd)

    @pl.when(step % 2 == 1)
    def _():
        body(proj_odd, proj_even)


def _mixer_in(x, seq, g, w_mix, mu, w0, wup_pad, a0, aup_pad, g_up, k_k, k_a, r_k, vmix,
              conv_w, conv_b, wax, bax, lam, hsum):
    n = x.shape[0]
    tm = MIX_TILE
    tiles = n // tm
    has_vmix = vmix is not None
    x_spec = pl.BlockSpec((tm, D_MODEL), lambda i: (jnp.minimum(i, tiles - 1), 0))
    lag512 = pl.BlockSpec((tm, RWKV_WIDTH), lambda i: (jnp.maximum(i - 1, 0), 0))
    lora = DECAY_LORA + AAA_LORA
    in_specs = [
        x_spec, _full((1, D_MODEL)), _full((D_MODEL, W_MIX)), _full((1, W_SHIFT)),
        _full((1, RWKV_WIDTH)), _full((lora, RWKV_WIDTH)), _full((1, RWKV_WIDTH)),
        _full((lora, RWKV_WIDTH)), _full((GATE_LORA, RWKV_WIDTH)),
        _full((1, RWKV_WIDTH)), _full((1, RWKV_WIDTH)), _full((1, RWKV_WIDTH)),
    ]
    args = [x, g, w_mix, mu, w0, wup_pad, a0, aup_pad, g_up, k_k, k_a, r_k]
    if has_vmix:
        v0, v_down, v_up, v_first = vmix
        in_specs += [_full((1, RWKV_WIDTH)), _full(v_down.shape), _full(v_up.shape), lag512]
        args += [v0, v_down, v_up, v_first]
    in_specs += [
        _full((CONV_WIDTH, LRU_WIDTH)), _full((1, LRU_WIDTH)), _full((LRU_WIDTH, 2 * LRU_WIDTH)),
        _full((1, 2 * LRU_WIDTH)), _full((1, LRU_WIDTH)), _full((RWKV_WIDTH, RWKV_WIDTH)),
    ]
    args += [conv_w, conv_b, wax, bax, lam, hsum]
    out512 = jax.ShapeDtypeStruct((n, RWKV_WIDTH), F32)
    return pl.pallas_call(
        functools.partial(_mixer_in_kernel, has_vmix, seq // tm),
        grid=(tiles + 1,),
        in_specs=in_specs,
        out_specs=[lag512] * 9,
        out_shape=[out512] * 9,
        scratch_shapes=[
            pltpu.VMEM((tm, W_MIX), F32),
            pltpu.VMEM((tm, W_MIX), F32),
            pltpu.VMEM((SUBLANES, W_SHIFT), F32),
            pltpu.VMEM((SUBLANES, LRU_WIDTH), F32),
            pltpu.VMEM((SUBLANES, LRU_WIDTH), F32),
        ],
        compiler_params=pltpu.CompilerParams(
            dimension_semantics=("arbitrary",), vmem_limit_bytes=VMEM_LIMIT),
        name="mixer_in",
    )(*args)


def _cumsum_rows(x):
    n, w = x.shape
    groups = n // SUBLANES
    x = x.reshape(groups, SUBLANES, w)
    sub = lax.broadcasted_iota(jnp.int32, x.shape, 1)
    d = 1
    while d < SUBLANES:
        x = x + jnp.where(sub >= d, pltpu.roll(x, d, 1), 0.0)
        d *= 2
    out = [x[0]]
    for i in range(1, groups):
        out.append(x[i] + out[-1][SUBLANES - 1:SUBLANES, :])
    return jnp.concatenate(out, axis=0)


def _rwkv_kernel(r_ref, lw_ref, k_ref, v_ref, kk_ref, a_ref, y_ref, s_ref):
    c = CHUNK
    nb = r_ref.shape[0]
    pairs = RWKV_WIDTH // LANES
    chains = [(bi, p) for bi in range(nb) for p in range(pairs)]

    @pl.when(pl.program_id(0) == 0)
    def _():
        s_ref[...] = jnp.zeros(s_ref.shape, F32)

    lane = lax.broadcasted_iota(jnp.int32, (c, LANES), 1)
    t_idx = lax.broadcasted_iota(jnp.int32, (c, LANES), 0)
    left = lane < HEAD_DIM
    s_idx = jnp.where(left, lane, lane - HEAD_DIM)
    strict = t_idx > s_idx
    incl = t_idx >= s_idx
    eye_w = (t_idx == s_idx).astype(F32)
    row2 = lax.broadcasted_iota(jnp.int32, (LANES, LANES), 0)
    lane2 = lax.broadcasted_iota(jnp.int32, (LANES, LANES), 1)
    same_head = (row2 < HEAD_DIM) == (lane2 < HEAD_DIM)
    eye2 = row2 == lane2

    def bd(w):
        return jnp.concatenate([jnp.where(left, w, 0.0), jnp.where(left, 0.0, w)], axis=0)

    def prep(g):
        ops = {}
        rows = slice(g * c, (g + 1) * c)
        for bi in range(nb):
            lw = lw_ref[bi, rows, :]
            cum = _cumsum_rows(lw)
            last = cum[c - 1:c, :]
            kk = kk_ref[bi, rows, :]
            k = k_ref[bi, rows, :]
            b = kk * a_ref[bi, rows, :]
            e_neg = jnp.exp(-cum)
            e_tail = jnp.exp(last - cum)
            a_t = -kk * jnp.exp(cum - lw)
            r_t = r_ref[bi, rows, :] * jnp.exp(cum)
            b_t = b * e_neg
            k_t = k * e_neg
            b_h = b * e_tail
            k_h = k * e_tail
            w_c = jnp.exp(last)
            v = v_ref[bi, rows, :]
            for p in range(pairs):
                sl = slice(p * LANES, (p + 1) * LANES)
                w_col = jnp.sum(jnp.where(eye2, w_c[:, sl], 0.0), axis=1, keepdims=True)
                ops[bi, p] = dict(a=a_t[:, sl], r=r_t[:, sl], bt=b_t[:, sl], kt=k_t[:, sl],
                                  bh=b_h[:, sl], kh=k_h[:, sl], v=v[:, sl], w_col=w_col)
        for ch in chains:
            o = ops[ch]
            m = _dot_nt(jnp.concatenate([o["a"], o["r"]], axis=0),
                        jnp.concatenate([bd(o["bt"]), bd(o["kt"])], axis=0))
            o["a_ab"] = jnp.where(strict, m[:c, :LANES], 0.0)
            o["a_ak"] = jnp.where(strict, m[:c, LANES:], 0.0)
            o["a_rb"] = jnp.where(incl, m[c:, :LANES], 0.0)
            o["a_rk"] = jnp.where(incl, m[c:, LANES:], 0.0)
        return ops

    def inverse_stages(ops):
        def first():
            for ch in chains:
                o = ops[ch]
                o["t"] = eye_w + o["a_ab"]
                o["p"] = _dot(o["a_ab"], bd(o["a_ab"]))

        def double(is_last):
            def run():
                for ch in chains:
                    o = ops[ch]
                    p_bd = bd(o["p"])
                    if is_last:
                        o["t"] = o["t"] + _dot(o["t"], p_bd)
                    else:
                        prod = _dot(jnp.concatenate([o["t"], o["p"]], axis=0), p_bd)
                        o["t"] = o["t"] + prod[:c]
                        o["p"] = prod[c:]
            return run

        steps = CHUNK.bit_length() - 2
        return [first] + [double(i == steps - 1) for i in range(steps)]

    def state_stages(ops, state, g):
        rows = slice(g * c, (g + 1) * c)

        def stage_x():
            for ch in chains:
                o = ops[ch]
                o["v_bd"] = bd(o["v"])
                o["x"] = _dot(jnp.concatenate([o["a"], o["a_ak"]], axis=1),
                              jnp.concatenate([state[ch], o["v_bd"]], axis=0))

        def stage_u():
            for ch in chains:
                o = ops[ch]
                o["u"] = _dot(o["t"], bd(o["x"]))

        def stage_s():
            for ch in chains:
                o = ops[ch]
                ds = _dot_tn(jnp.concatenate([o["bh"], o["kh"]], axis=0),
                             jnp.concatenate([o["u"], o["v"]], axis=0))
                o["s_new"] = o["w_col"] * state[ch] + jnp.where(same_head, ds, 0.0)

        def stage_y():
            for ch in chains:
                o = ops[ch]
                bi, p = ch
                y_ref[bi, rows, p * LANES:(p + 1) * LANES] = _dot(
                    jnp.concatenate([o["r"], o["a_rb"], o["a_rk"]], axis=1),
                    jnp.concatenate([state[ch], bd(o["u"]), o["v_bd"]], axis=0))
                state[ch] = o["s_new"]

        return [stage_x, stage_u, stage_s, stage_y]

    state = {ch: s_ref[ch[0], ch[1]] for ch in chains}
    ops_next = prep(0)
    pending = []
    for g in range(RWKV_CHUNKS):
        ops_cur = ops_next
        inv = inverse_stages(ops_cur)
        half = len(inv) // 2
        for i, stage in enumerate(inv):
            stage()
            if i < len(pending):
                pending[i]()
            if i == half and g + 1 < RWKV_CHUNKS:
                ops_next = prep(g + 1)
        for stage in pending[len(inv):]:
            stage()
        pending = state_stages(ops_cur, state, g)
    for stage in pending:
        stage()
    for ch, s_val in state.items():
        s_ref[ch[0], ch[1]] = s_val


def _rwkv(r, lw, k, v, kk, a, batch, seq):
    rows = RWKV_CHUNKS * CHUNK
    spec = pl.BlockSpec((batch, rows, RWKV_WIDTH), lambda i: (0, i, 0))
    shape3 = (batch, seq, RWKV_WIDTH)
    y = pl.pallas_call(
        _rwkv_kernel,
        grid=(seq // rows,),
        in_specs=[spec] * 6,
        out_specs=spec,
        out_shape=jax.ShapeDtypeStruct(shape3, F32),
        scratch_shapes=[pltpu.VMEM((batch, RWKV_WIDTH // LANES, LANES, LANES), F32)],
        compiler_params=pltpu.CompilerParams(
            dimension_semantics=("arbitrary",), vmem_limit_bytes=VMEM_LIMIT),
        name="rwkv",
    )(*(t.reshape(shape3) for t in (r, lw, k, v, kk, a)))
    return y.reshape(batch * seq, RWKV_WIDTH)


def _mixer_out_kernel(x_ref, y_ref, bonus_ref, g_ref, ob_ref, gpre_ref, wgate_ref, lnw_ref, lnb_ref,
                      prw_ref, plr_ref, wout_ref, gn_ref, hmean_ref, o_ref):
    x = x_ref[...]
    gates = jax.nn.sigmoid(_dot(_rmsnorm(x, gpre_ref[...]), wgate_ref[...]))
    y = y_ref[...]
    mu = _dot(y, hmean_ref[...])
    yc = y - mu
    var = _dot(yc * yc, hmean_ref[...])
    yn = yc * lax.rsqrt(var + GN_EPS) * lnw_ref[...] + lnb_ref[...] + bonus_ref[...]
    o_a = yn * g_ref[...]
    merged = (gates[:, :D_MODEL] * _dot(o_a, prw_ref[...])
              + gates[:, D_MODEL:] * _dot(ob_ref[...], plr_ref[...]))
    out = _dot(merged, wout_ref[...])
    o_ref[...] = x + _rmsnorm(out, gn_ref[...])


def _mixer_out(x, y, bonus, g, o_b, g_pre, w_gate, ln_w, ln_b, p_rwkv, p_lru, w_out, gn, hmean):
    n = x.shape[0]
    tm = MIX_TILE
    row512 = pl.BlockSpec((tm, RWKV_WIDTH), lambda i: (i, 0))
    row1024 = pl.BlockSpec((tm, D_MODEL), lambda i: (i, 0))
    return pl.pallas_call(
        _mixer_out_kernel,
        grid=(n // tm,),
        in_specs=[row1024, row512, row512, row512, row512,
                  _full((1, D_MODEL)), _full((D_MODEL, 2 * D_MODEL)),
                  _full((1, RWKV_WIDTH)), _full((1, RWKV_WIDTH)),
                  _full((RWKV_WIDTH, D_MODEL)), _full((LRU_WIDTH, D_MODEL)),
                  _full((D_MODEL, D_MODEL)), _full((1, D_MODEL)),
                  _full((RWKV_WIDTH, RWKV_WIDTH))],
        out_specs=row1024,
        out_shape=jax.ShapeDtypeStruct((n, D_MODEL), F32),
        compiler_params=pltpu.CompilerParams(
            dimension_semantics=("arbitrary",), vmem_limit_bytes=VMEM_LIMIT),
        name="mixer_out",
    )(x, y, bonus, g, o_b, g_pre, w_gate, ln_w, ln_b, p_rwkv, p_lru, w_out, gn, hmean)


def _block_diag(w):
    nb, d, e = w.shape
    eye = jnp.eye(nb, dtype=w.dtype)
    return jnp.einsum("nde,nm->ndme", w, eye).reshape(nb * d, nb * e)


def kernel(x, norm_g, ffn1_w_up, ffn1_w_down, ffn2_w_up, ffn2_w_down, w_in, shift_mu, w0, w_up, a0, a_up, g_up, k_k, k_a, r_k, ln_w, ln_b, v0, v_down, v_up, conv_w, conv_b, lru_wa, lru_ba, lru_wx, lru_bx, lru_lambda, p_rwkv, p_lru, w_out):
    batch, seq, d = x.shape
    depth = norm_g.shape[0]
    assert d == D_MODEL and seq % MIX_TILE == 0 and (batch * seq) % FFN_TILE == 0
    assert seq % (RWKV_CHUNKS * CHUNK) == 0
    xf = x.reshape(batch * seq, d)
    bf = lambda t: t.astype(BF16)
    row = lambda t: t.reshape(1, -1)
    head = jnp.arange(RWKV_WIDTH) // HEAD_DIM
    same = (head[:, None] == head[None, :])
    hsum = same.astype(BF16)
    hmean = (same.astype(F32) / HEAD_DIM).astype(BF16)
    zeros_lora = jnp.zeros((DECAY_LORA, RWKV_WIDTH), F32)
    v_first = None
    for l in range(depth):
        g = norm_g[l]
        xf = _ffn(xf, g[0:2], bf(ffn1_w_up[l]), bf(ffn1_w_down[l]))
        vmix = None if l == 0 else (row(v0[l - 1]), bf(v_down[l - 1]), bf(v_up[l - 1]), v_first)
        wup_pad = bf(jnp.concatenate([w_up[l], zeros_lora], axis=0))
        aup_pad = bf(jnp.concatenate([zeros_lora, a_up[l]], axis=0))
        wax = bf(jnp.concatenate([_block_diag(lru_wa[l]), _block_diag(lru_wx[l])], axis=1))
        bax = row(jnp.concatenate([lru_ba[l], lru_bx[l]]))
        w_in_l = bf(w_in[l])
        r, lw, k, v, kk, a, bonus, gate, o_b = _mixer_in(
            xf, seq, row(g[2]), w_in_l[:, :W_MIX], row(shift_mu[l]), row(w0[l]), wup_pad, row(a0[l]),
            aup_pad, bf(g_up[l]), row(k_k[l]), row(k_a[l]), row(r_k[l]), vmix,
            conv_w[l], row(conv_b[l]), wax, bax, row(lru_lambda[l]), hsum)
        if l == 0:
            v_first = v
        y = _rwkv(r, lw, k, v, kk, a, batch, seq)
        xf = _mixer_out(xf, y, bonus, gate, o_b, row(g[2]), w_in_l[:, W_MIX:], row(ln_w[l]), row(ln_b[l]),
                        bf(p_rwkv[l]), bf(p_lru[l]), bf(w_out[l]), row(g[3]), hmean)
        xf = _ffn(xf, g[4:6], bf(ffn2_w_up[l]), bf(ffn2_w_down[l]))
    return xf.reshape(batch, seq, d)
```

```python
import functools

import jax
import jax.numpy as jnp
from jax import lax
from jax.experimental import pallas as pl
from jax.experimental.pallas import tpu as pltpu

D_MODEL = 1024
RWKV_WIDTH = 512
HEAD_DIM = 64
LRU_WIDTH = 512
LRU_BLOCKS = 8
CONV_WIDTH = 4
LRU_C = 8.0
D_FF = 2816
RMS_EPS = 1e-6
GN_EPS = 64e-5
FFN1_NORM_ROW = 0
MIXER_NORM_ROW = 2
FFN2_NORM_ROW = 4
DECAY_LORA = 64
AAA_LORA = 64
GATE_LORA = 128
W_SHIFT = 3 * RWKV_WIDTH + DECAY_LORA + AAA_LORA + GATE_LORA
W_MIX = W_SHIFT + 2 * LRU_WIDTH
W_IN = W_MIX + 2 * D_MODEL

RWKV_CHUNKS = 8
CHUNK = 64
LANES = 128
SUBLANES = 8
FFN_TILE = 512
FFN_COLS = 256
MIX_TILE = 256
MIX_OUT_TILE = 512
PROJ_COLS = 256
PROJ_SPLIT = (2, 2, 2, 2, 3)
VMEM_LIMIT = 56 * 1024 * 1024

BF16 = jnp.bfloat16
F32 = jnp.float32


def _dot(a, b):
    return jnp.dot(a.astype(BF16), b.astype(BF16), preferred_element_type=F32)


def _dot_nt(a, b):
    return lax.dot_general(a.astype(BF16), b.astype(BF16), (((1,), (1,)), ((), ())),
                           preferred_element_type=F32)


def _dot_tn(a, b):
    return lax.dot_general(a.astype(BF16), b.astype(BF16), (((0,), (0,)), ((), ())),
                           preferred_element_type=F32)


def _rmsnorm(x, g):
    ms = jnp.mean(x * x, axis=-1, keepdims=True)
    return x * lax.rsqrt(ms + RMS_EPS) * g


def _softplus(x, accurate):
    e = jnp.exp(-jnp.abs(x))
    return jnp.maximum(x, 0.0) + (jnp.log1p(e) if accurate else jnp.log(1.0 + e))


def _shift_rows(prev_rows, x, d):
    ext = jnp.concatenate([prev_rows, x], axis=0)
    return pltpu.roll(ext, d, 0)[SUBLANES:]


def _full(shape):
    return pl.BlockSpec(shape, lambda *_: (0,) * len(shape))


def _layer(arr, l):
    shape = arr.shape[1:]
    return pl.BlockSpec((None,) + shape, lambda *_: (l,) + (0,) * len(shape))


def _ffn_kernel(norm_row, x_ref, g_ref, wup_ref, wdn_ref, o_ref):
    x = x_ref[...]
    h = _rmsnorm(x, g_ref[norm_row:norm_row + 1, :]).astype(BF16)
    acc = jnp.zeros(x.shape, F32)
    for j in range(D_FF // FFN_COLS):
        lo = j * FFN_COLS
        gate = jnp.dot(h, wup_ref[:, lo:lo + FFN_COLS], preferred_element_type=F32)
        up = jnp.dot(h, wup_ref[:, D_FF + lo:D_FF + lo + FFN_COLS], preferred_element_type=F32)
        act = (gate * jax.nn.sigmoid(gate) * up).astype(BF16)
        acc = acc + jnp.dot(act, wdn_ref[lo:lo + FFN_COLS, :], preferred_element_type=F32)
    o_ref[...] = x + 0.5 * _rmsnorm(acc, g_ref[norm_row + 1:norm_row + 2, :])


def _ffn(x, l, norm_row, norm_g, w_up, w_down):
    n = x.shape[0]
    return pl.pallas_call(
        functools.partial(_ffn_kernel, norm_row),
        grid=(n // FFN_TILE,),
        in_specs=[
            pl.BlockSpec((FFN_TILE, D_MODEL), lambda i: (i, 0)),
            _layer(norm_g, l), _layer(w_up, l), _layer(w_down, l),
        ],
        out_specs=pl.BlockSpec((FFN_TILE, D_MODEL), lambda i: (i, 0)),
        out_shape=jax.ShapeDtypeStruct((n, D_MODEL), F32),
        compiler_params=pltpu.CompilerParams(
            dimension_semantics=("arbitrary",), vmem_limit_bytes=VMEM_LIMIT),
        name="ffn",
    )(x, norm_g, w_up, w_down)


def _scan_rows(a, b, h0):
    n, w = a.shape
    groups = n // SUBLANES
    a = a.reshape(groups, SUBLANES, w)
    b = b.reshape(groups, SUBLANES, w)
    sub = lax.broadcasted_iota(jnp.int32, a.shape, 1)
    d = 1
    while d < SUBLANES:
        keep = sub >= d
        a_prev = jnp.where(keep, pltpu.roll(a, d, 1), 1.0)
        b_prev = jnp.where(keep, pltpu.roll(b, d, 1), 0.0)
        b = a * b_prev + b
        a = a * a_prev
        d *= 2
    out = []
    for i in range(groups):
        hi = a[i] * h0 + b[i]
        out.append(hi)
        h0 = hi[SUBLANES - 1:SUBLANES, :]
    return jnp.concatenate(out, axis=0)


def _mixer_in_kernel(has_vmix, tiles_per_seq, *refs):
    (x_ref, g_ref, win_ref, mu_ref, w0_ref, wup_ref, a0_ref, aup_ref, gup_ref,
     kk_ref, ka_ref, rk_ref) = refs[:12]
    refs = refs[12:]
    if has_vmix:
        v0_ref, vdown_ref, vup_ref, vfirst_ref = refs[:4]
        refs = refs[4:]
    (convw_ref, convb_ref, wax_ref, bax_ref, lam_ref, hsum_ref,
     r_out, lw_out, k_out, v_out, kkn_out, a_out, bonus_out, g_out, ob_out,
     proj_even, proj_odd, pcarry, xcarry, hcarry) = refs

    tm = x_ref.shape[0]
    step = pl.program_id(0)
    first = (step + tiles_per_seq - 1) % tiles_per_seq == 0

    @pl.when(step == 0)
    def _():
        proj_odd[...] = jnp.zeros(proj_odd.shape, F32)

    @pl.when(jnp.logical_or(first, step == 0))
    def _():
        pcarry[...] = jnp.zeros(pcarry.shape, F32)
        xcarry[...] = jnp.zeros(xcarry.shape, F32)
        hcarry[...] = jnp.zeros(hcarry.shape, F32)

    def body(proj_new, proj):
        h = _rmsnorm(x_ref[...], g_ref[MIXER_NORM_ROW:MIXER_NORM_ROW + 1, :]).astype(BF16)
        chunk_starts = iter(range(0, W_MIX, PROJ_COLS))

        def project(n_chunks):
            for _ in range(n_chunks):
                lo = next(chunk_starts)
                proj_new[:, lo:lo + PROJ_COLS] = jnp.dot(
                    h, win_ref[:, lo:lo + PROJ_COLS], preferred_element_type=F32)

        project(PROJ_SPLIT[0])
        p = proj[:, :W_SHIFT]
        prev = _shift_rows(pcarry[...], p, 1)
        pcarry[...] = p[tm - SUBLANES:, :]
        p = p + mu_ref[...] * (prev - p)
        rw = RWKV_WIDTH
        r, k, v = p[:, 0:rw], p[:, rw:2 * rw], p[:, 2 * rw:3 * rw]
        xwa = p[:, 3 * rw:3 * rw + DECAY_LORA + AAA_LORA]
        xg = p[:, 3 * rw + DECAY_LORA + AAA_LORA:]
        w = -_softplus(-(w0_ref[...] + _dot(jnp.tanh(xwa), wup_ref[...])), accurate=False) - 0.5
        a = jax.nn.sigmoid(a0_ref[...] + _dot(xwa, aup_ref[...]))
        g_out[...] = _dot(jax.nn.sigmoid(xg), gup_ref[...])
        if has_vmix:
            v_lora = _dot(v, vdown_ref[...])
        project(PROJ_SPLIT[1])
        lw_out[...] = -jnp.exp(w)
        if has_vmix:
            mix = jax.nn.sigmoid(v0_ref[...] + _dot(v_lora, vup_ref[...]))
            v = v + (vfirst_ref[...] - v) * mix
        kk = k * kk_ref[...]
        kk = kk * lax.rsqrt(jnp.maximum(_dot(kk * kk, hsum_ref[...]), 1e-24))
        project(PROJ_SPLIT[2])
        k = k * (1.0 + (a - 1.0) * ka_ref[...])
        r_out[...] = r
        k_out[...] = k
        v_out[...] = v
        kkn_out[...] = kk
        a_out[...] = a
        bonus_out[...] = _dot(r * k * rk_ref[...], hsum_ref[...]) * v
        project(PROJ_SPLIT[3])

        lw_ = LRU_WIDTH
        lx = proj[:, W_SHIFT:W_SHIFT + lw_]
        lgate = proj[:, W_SHIFT + lw_:]
        xprev = xcarry[...]
        xb = convb_ref[...] + convw_ref[CONV_WIDTH - 1:CONV_WIDTH, :] * lx
        for d in range(1, CONV_WIDTH):
            xb = xb + convw_ref[CONV_WIDTH - 1 - d:CONV_WIDTH - d, :] * _shift_rows(xprev, lx, d)
        xcarry[...] = lx[tm - SUBLANES:, :]
        gates = _dot(xb, wax_ref[...]) + bax_ref[...]
        project(PROJ_SPLIT[4])
        gate_a = jax.nn.sigmoid(gates[:, :lw_])
        gate_x = jax.nn.sigmoid(gates[:, lw_:])
        log_a = -LRU_C * gate_a * _softplus(-lam_ref[...], accurate=True)
        a_l = jnp.exp(log_a)
        mult = jnp.sqrt(-jnp.tanh(log_a) * (a_l * a_l + 1.0))
        row = lax.broadcasted_iota(jnp.int32, log_a.shape, 0)
        seq_start = row + jnp.where(first, 0, 1) == 0
        mult = jnp.where(seq_start, 1.0, mult)
        hseq = _scan_rows(a_l, xb * gate_x * mult, hcarry[SUBLANES - 1:SUBLANES, :])
        hcarry[...] = hseq[tm - SUBLANES:, :]
        ob_out[...] = hseq * jax.nn.gelu(lgate)

    @pl.when(step % 2 == 0)
    def _():
        body(proj_even, proj_odd)

    @pl.when(step % 2 == 1)
    def _():
        body(proj_odd, proj_even)


def _mixer_in(x, seq, l, norm_g, prm, v_first, hsum):
    n = x.shape[0]
    tm = MIX_TILE
    tiles = n // tm
    has_vmix = v_first is not None
    x_spec = pl.BlockSpec((tm, D_MODEL), lambda i: (jnp.minimum(i, tiles - 1), 0))
    lag512 = pl.BlockSpec((tm, RWKV_WIDTH), lambda i: (jnp.maximum(i - 1, 0), 0))
    names = ["w_mix", "shift_mu", "w0", "wup_pad", "a0", "aup_pad", "g_up", "k_k", "k_a", "r_k"]
    in_specs = [x_spec, _layer(norm_g, l)] + [_layer(prm[k], l) for k in names]
    args = [x, norm_g] + [prm[k] for k in names]
    if has_vmix:
        vnames = ["v0", "v_down", "v_up"]
        in_specs += [_layer(prm[k], l - 1) for k in vnames] + [lag512]
        args += [prm[k] for k in vnames] + [v_first]
    names = ["conv_w", "conv_b", "wax", "bax", "lru_lambda"]
    in_specs += [_layer(prm[k], l) for k in names] + [_full((RWKV_WIDTH, RWKV_WIDTH))]
    args += [prm[k] for k in names] + [hsum]
    out512 = jax.ShapeDtypeStruct((n, RWKV_WIDTH), F32)
    return pl.pallas_call(
        functools.partial(_mixer_in_kernel, has_vmix, seq // tm),
        grid=(tiles + 1,),
        in_specs=in_specs,
        out_specs=[lag512] * 9,
        out_shape=[out512] * 9,
        scratch_shapes=[
            pltpu.VMEM((tm, W_MIX), F32),
            pltpu.VMEM((tm, W_MIX), F32),
            pltpu.VMEM((SUBLANES, W_SHIFT), F32),
            pltpu.VMEM((SUBLANES, LRU_WIDTH), F32),
            pltpu.VMEM((SUBLANES, LRU_WIDTH), F32),
        ],
        compiler_params=pltpu.CompilerParams(
            dimension_semantics=("arbitrary",), vmem_limit_bytes=VMEM_LIMIT),
        name="mixer_in",
    )(*args)


def _cumsum_rows(x):
    n, w = x.shape
    groups = n // SUBLANES
    x = x.reshape(groups, SUBLANES, w)
    sub = lax.broadcasted_iota(jnp.int32, x.shape, 1)
    d = 1
    while d < SUBLANES:
        x = x + jnp.where(sub >= d, pltpu.roll(x, d, 1), 0.0)
        d *= 2
    out = [x[0]]
    for i in range(1, groups):
        out.append(x[i] + out[-1][SUBLANES - 1:SUBLANES, :])
    return jnp.concatenate(out, axis=0)


def _rwkv_kernel(r_ref, lw_ref, k_ref, v_ref, kk_ref, a_ref, y_ref, s_ref):
    c = CHUNK
    nb = r_ref.shape[0]
    pairs = RWKV_WIDTH // LANES
    chains = [(bi, p) for bi in range(nb) for p in range(pairs)]

    @pl.when(pl.program_id(0) == 0)
    def _():
        s_ref[...] = jnp.zeros(s_ref.shape, F32)

    lane = lax.broadcasted_iota(jnp.int32, (c, LANES), 1)
    t_idx = lax.broadcasted_iota(jnp.int32, (c, LANES), 0)
    left = lane < HEAD_DIM
    s_idx = jnp.where(left, lane, lane - HEAD_DIM)
    strict = t_idx > s_idx
    incl = t_idx >= s_idx
    eye_w = (t_idx == s_idx).astype(F32)
    row2 = lax.broadcasted_iota(jnp.int32, (LANES, LANES), 0)
    lane2 = lax.broadcasted_iota(jnp.int32, (LANES, LANES), 1)
    same_head = (row2 < HEAD_DIM) == (lane2 < HEAD_DIM)
    eye2 = row2 == lane2

    def bd(w):
        return jnp.concatenate([jnp.where(left, w, 0.0), jnp.where(left, 0.0, w)], axis=0)

    def prep(g):
        ops = {}
        rows = slice(g * c, (g + 1) * c)
        for bi in range(nb):
            lw = lw_ref[bi, rows, :]
            cum = _cumsum_rows(lw)
            last = cum[c - 1:c, :]
            kk = kk_ref[bi, rows, :]
            k = k_ref[bi, rows, :]
            b = kk * a_ref[bi, rows, :]
            e_neg = jnp.exp(-cum)
            e_tail = jnp.exp(last - cum)
            a_t = -kk * jnp.exp(cum - lw)
            r_t = r_ref[bi, rows, :] * jnp.exp(cum)
            b_t = b * e_neg
            k_t = k * e_neg
            b_h = b * e_tail
            k_h = k * e_tail
            w_c = jnp.exp(last)
            v = v_ref[bi, rows, :]
            for p in range(pairs):
                sl = slice(p * LANES, (p + 1) * LANES)
                w_col = jnp.sum(jnp.where(eye2, w_c[:, sl], 0.0), axis=1, keepdims=True)
                ops[bi, p] = dict(a=a_t[:, sl], r=r_t[:, sl], bt=b_t[:, sl], kt=k_t[:, sl],
                                  bh=b_h[:, sl], kh=k_h[:, sl], v=v[:, sl], w_col=w_col)
        for ch in chains:
            o = ops[ch]
            m = _dot_nt(jnp.concatenate([o["a"], o["r"]], axis=0),
                        jnp.concatenate([bd(o["bt"]), bd(o["kt"])], axis=0))
            o["a_ab"] = jnp.where(strict, m[:c, :LANES], 0.0)
            o["a_ak"] = jnp.where(strict, m[:c, LANES:], 0.0)
            o["a_rb"] = jnp.where(incl, m[c:, :LANES], 0.0)
            o["a_rk"] = jnp.where(incl, m[c:, LANES:], 0.0)
        return ops

    def inverse_stages(ops):
        def first():
            for ch in chains:
                o = ops[ch]
                o["t"] = eye_w + o["a_ab"]
                o["p"] = _dot(o["a_ab"], bd(o["a_ab"]))

        def double(is_last):
            def run():
                for ch in chains:
                    o = ops[ch]
                    p_bd = bd(o["p"])
                    if is_last:
                        o["t"] = o["t"] + _dot(o["t"], p_bd)
                    else:
                        prod = _dot(jnp.concatenate([o["t"], o["p"]], axis=0), p_bd)
                        o["t"] = o["t"] + prod[:c]
                        o["p"] = prod[c:]
            return run

        steps = CHUNK.bit_length() - 2
        return [first] + [double(i == steps - 1) for i in range(steps)]

    def state_stages(ops, state, g):
        rows = slice(g * c, (g + 1) * c)

        def stage_x():
            for ch in chains:
                o = ops[ch]
                o["v_bd"] = bd(o["v"])
                o["x"] = _dot(jnp.concatenate([o["a"], o["a_ak"]], axis=1),
                              jnp.concatenate([state[ch], o["v_bd"]], axis=0))

        def stage_u():
            for ch in chains:
                o = ops[ch]
                o["u"] = _dot(o["t"], bd(o["x"]))

        def stage_s():
            for ch in chains:
                o = ops[ch]
                ds = _dot_tn(jnp.concatenate([o["bh"], o["kh"]], axis=0),
                             jnp.concatenate([o["u"], o["v"]], axis=0))
                o["s_new"] = o["w_col"] * state[ch] + jnp.where(same_head, ds, 0.0)

        def stage_y():
            for ch in chains:
                o = ops[ch]
                bi, p = ch
                y_ref[bi, rows, p * LANES:(p + 1) * LANES] = _dot(
                    jnp.concatenate([o["r"], o["a_rb"], o["a_rk"]], axis=1),
                    jnp.concatenate([state[ch], bd(o["u"]), o["v_bd"]], axis=0))
                state[ch] = o["s_new"]

        return [stage_x, stage_u, stage_s, stage_y]

    state = {ch: s_ref[ch[0], ch[1]] for ch in chains}
    ops_next = prep(0)
    pending = []
    for g in range(RWKV_CHUNKS):
        ops_cur = ops_next
        inv = inverse_stages(ops_cur)
        half = len(inv) // 2
        for i, stage in enumerate(inv):
            stage()
            if i < len(pending):
                pending[i]()
            if i == half and g + 1 < RWKV_CHUNKS:
                ops_next = prep(g + 1)
        for stage in pending[len(inv):]:
            stage()
        pending = state_stages(ops_cur, state, g)
    for stage in pending:
        stage()
    for ch, s_val in state.items():
        s_ref[ch[0], ch[1]] = s_val


def _rwkv(r, lw, k, v, kk, a, batch, seq):
    rows = RWKV_CHUNKS * CHUNK
    spec = pl.BlockSpec((batch, rows, RWKV_WIDTH), lambda i: (0, i, 0))
    shape3 = (batch, seq, RWKV_WIDTH)
    y = pl.pallas_call(
        _rwkv_kernel,
        grid=(seq // rows,),
        in_specs=[spec] * 6,
        out_specs=spec,
        out_shape=jax.ShapeDtypeStruct(shape3, F32),
        scratch_shapes=[pltpu.VMEM((batch, RWKV_WIDTH // LANES, LANES, LANES), F32)],
        compiler_params=pltpu.CompilerParams(
            dimension_semantics=("arbitrary",), vmem_limit_bytes=VMEM_LIMIT),
        name="rwkv",
    )(*(t.reshape(shape3) for t in (r, lw, k, v, kk, a)))
    return y.reshape(batch * seq, RWKV_WIDTH)


def _mixer_out_kernel(x_ref, y_ref, bonus_ref, g_ref, ob_ref, gnorm_ref, wgate_ref, lnw_ref, lnb_ref,
                      prw_ref, plr_ref, wout_ref, hmean_ref, o_ref):
    x = x_ref[...]
    pre = gnorm_ref[MIXER_NORM_ROW:MIXER_NORM_ROW + 1, :]
    gates = jax.nn.sigmoid(_dot(_rmsnorm(x, pre), wgate_ref[...]))
    y = y_ref[...]
    mu = _dot(y, hmean_ref[...])
    yc = y - mu
    var = _dot(yc * yc, hmean_ref[...])
    yn = yc * lax.rsqrt(var + GN_EPS) * lnw_ref[...] + lnb_ref[...] + bonus_ref[...]
    o_a = yn * g_ref[...]
    merged = (gates[:, :D_MODEL] * _dot(o_a, prw_ref[...])
              + gates[:, D_MODEL:] * _dot(ob_ref[...], plr_ref[...]))
    out = _dot(merged, wout_ref[...])
    o_ref[...] = x + _rmsnorm(out, gnorm_ref[MIXER_NORM_ROW + 1:MIXER_NORM_ROW + 2, :])


def _mixer_out(x, y, bonus, g, o_b, l, norm_g, prm, hmean):
    n = x.shape[0]
    tm = MIX_OUT_TILE
    row512 = pl.BlockSpec((tm, RWKV_WIDTH), lambda i: (i, 0))
    row1024 = pl.BlockSpec((tm, D_MODEL), lambda i: (i, 0))
    names = ["w_gate", "ln_w", "ln_b", "p_rwkv", "p_lru", "w_out"]
    return pl.pallas_call(
        _mixer_out_kernel,
        grid=(n // tm,),
        in_specs=([row1024, row512, row512, row512, row512, _layer(norm_g, l)]
                  + [_layer(prm[k], l) for k in names] + [_full((RWKV_WIDTH, RWKV_WIDTH))]),
        out_specs=row1024,
        out_shape=jax.ShapeDtypeStruct((n, D_MODEL), F32),
        compiler_params=pltpu.CompilerParams(
            dimension_semantics=("arbitrary",), vmem_limit_bytes=VMEM_LIMIT),
        name="mixer_out",
    )(x, y, bonus, g, o_b, norm_g, *[prm[k] for k in names], hmean)


def _block_diag(w):
    nl, nb, d, e = w.shape
    eye = jnp.eye(nb, dtype=w.dtype)
    return jnp.einsum("lnde,nm->lndme", w, eye).reshape(nl, nb * d, nb * e)


def kernel(x, norm_g, ffn1_w_up, ffn1_w_down, ffn2_w_up, ffn2_w_down, w_in, shift_mu, w0, w_up, a0, a_up, g_up, k_k, k_a, r_k, ln_w, ln_b, v0, v_down, v_up, conv_w, conv_b, lru_wa, lru_ba, lru_wx, lru_bx, lru_lambda, p_rwkv, p_lru, w_out):
    batch, seq, d = x.shape
    depth = norm_g.shape[0]
    assert d == D_MODEL and seq % MIX_TILE == 0 and (batch * seq) % FFN_TILE == 0
    assert seq % (RWKV_CHUNKS * CHUNK) == 0 and (batch * seq) % MIX_OUT_TILE == 0
    xf = x.reshape(batch * seq, d)
    bf = lambda t: t.astype(BF16)
    rows = lambda t: t.reshape(t.shape[0], 1, -1)
    head = jnp.arange(RWKV_WIDTH) // HEAD_DIM
    same = (head[:, None] == head[None, :])
    hsum = same.astype(BF16)
    hmean = (same.astype(F32) / HEAD_DIM).astype(BF16)
    zeros_lora = jnp.zeros((depth, DECAY_LORA, RWKV_WIDTH), F32)
    prm = dict(
        w_mix=bf(w_in[:, :, :W_MIX]), w_gate=bf(w_in[:, :, W_MIX:]),
        shift_mu=rows(shift_mu), w0=rows(w0), a0=rows(a0), k_k=rows(k_k), k_a=rows(k_a), r_k=rows(r_k),
        wup_pad=bf(jnp.concatenate([w_up, zeros_lora], axis=1)),
        aup_pad=bf(jnp.concatenate([zeros_lora, a_up], axis=1)),
        g_up=bf(g_up), v0=rows(v0), v_down=bf(v_down), v_up=bf(v_up),
        conv_w=conv_w, conv_b=rows(conv_b),
        wax=bf(jnp.concatenate([_block_diag(lru_wa), _block_diag(lru_wx)], axis=2)),
        bax=rows(jnp.concatenate([lru_ba, lru_bx], axis=1)), lru_lambda=rows(lru_lambda),
        ln_w=rows(ln_w), ln_b=rows(ln_b), p_rwkv=bf(p_rwkv), p_lru=bf(p_lru), w_out=bf(w_out),
    )
    ffn_w = [(bf(ffn1_w_up), bf(ffn1_w_down)), (bf(ffn2_w_up), bf(ffn2_w_down))]
    v_first = None
    for l in range(depth):
        xf = _ffn(xf, l, FFN1_NORM_ROW, norm_g, *ffn_w[0])
        r, lw, k, v, kk, a, bonus, gate, o_b = _mixer_in(xf, seq, l, norm_g, prm, v_first, hsum)
        if l == 0:
            v_first = v
        y = _rwkv(r, lw, k, v, kk, a, batch, seq)
        xf = _mixer_out(xf, y, bonus, gate, o_b, l, norm_g, prm, hmean)
        xf = _ffn(xf, l, FFN2_NORM_ROW, norm_g, *ffn_w[1])
    return xf.reshape(batch, seq, d)
```

```python
import functools

import jax
import jax.numpy as jnp
from jax import lax
from jax.experimental import pallas as pl
from jax.experimental.pallas import tpu as pltpu

D_MODEL = 1024
RWKV_WIDTH = 512
HEAD_DIM = 64
LRU_WIDTH = 512
LRU_BLOCKS = 8
CONV_WIDTH = 4
LRU_C = 8.0
D_FF = 2816
RMS_EPS = 1e-6
GN_EPS = 64e-5
FFN1_NORM_ROW = 0
MIXER_NORM_ROW = 2
FFN2_NORM_ROW = 4
DECAY_LORA = 64
AAA_LORA = 64
GATE_LORA = 128
W_SHIFT = 3 * RWKV_WIDTH + DECAY_LORA + AAA_LORA + GATE_LORA
W_MIX = W_SHIFT + 2 * LRU_WIDTH
W_IN = W_MIX + 2 * D_MODEL

RWKV_CHUNKS = 8
CHUNK = 64
LANES = 128
SUBLANES = 8
BF16_ROWS = 16
FFN_TILE = 512
FFN_COLS = 256
MIX_TILE = 256
MIX_OUT_TILE = 512
PROJ_COLS = 256
PROJ_SPLIT = (2, 2, 2, 2, 3)
VMEM_LIMIT = 56 * 1024 * 1024

BF16 = jnp.bfloat16
F32 = jnp.float32


def _dot(a, b):
    return jnp.dot(a.astype(BF16), b.astype(BF16), preferred_element_type=F32)


def _dot_nt(a, b):
    return lax.dot_general(a.astype(BF16), b.astype(BF16), (((1,), (1,)), ((), ())),
                           preferred_element_type=F32)


def _dot_tn(a, b):
    return lax.dot_general(a.astype(BF16), b.astype(BF16), (((0,), (0,)), ((), ())),
                           preferred_element_type=F32)


def _rmsnorm(x, g):
    ms = jnp.mean(x * x, axis=-1, keepdims=True)
    return x * lax.rsqrt(ms + RMS_EPS) * g


def _softplus(x, accurate):
    e = jnp.exp(-jnp.abs(x))
    return jnp.maximum(x, 0.0) + (jnp.log1p(e) if accurate else jnp.log(1.0 + e))


def _shift_rows(prev_rows, x, d):
    ext = jnp.concatenate([prev_rows, x], axis=0)
    return pltpu.roll(ext, d, 0)[SUBLANES:]


def _full(shape):
    return pl.BlockSpec(shape, lambda *_: (0,) * len(shape))


def _resident(shape):
    return pl.BlockSpec(shape, lambda *_: (0,) * len(shape), pipeline_mode=pl.Buffered(1))


def _layer(arr, l):
    shape = arr.shape[1:]
    return pl.BlockSpec((None,) + shape, lambda *_: (l,) + (0,) * len(shape),
                        pipeline_mode=pl.Buffered(1))


def _cast_job(src, l, steps):
    _, rows, cols = src.shape
    blk = next(b for b in range(BF16_ROWS, rows + 1, BF16_ROWS)
               if rows % b == 0 and b * steps >= rows)
    last = rows // blk - 1
    in_spec = pl.BlockSpec((None, blk, cols), lambda i: (l, jnp.minimum(i, last), 0))
    out_spec = pl.BlockSpec((blk, cols), lambda i: (jnp.minimum(i, last), 0))
    return in_spec, out_spec, jax.ShapeDtypeStruct((rows, cols), BF16)


def _ffn_kernel(norm_row, n_cast, x_ref, g_ref, wup_ref, wdn_ref, *refs):
    cast_src, o_ref, cast_dst = refs[:n_cast], refs[n_cast], refs[n_cast + 1:]
    x = x_ref[...]
    h = _rmsnorm(x, g_ref[norm_row:norm_row + 1, :]).astype(BF16)
    acc = jnp.zeros(x.shape, F32)
    for j in range(D_FF // FFN_COLS):
        lo = j * FFN_COLS
        gate = jnp.dot(h, wup_ref[:, lo:lo + FFN_COLS], preferred_element_type=F32)
        up = jnp.dot(h, wup_ref[:, D_FF + lo:D_FF + lo + FFN_COLS], preferred_element_type=F32)
        act = (gate * jax.nn.sigmoid(gate) * up).astype(BF16)
        acc = acc + jnp.dot(act, wdn_ref[lo:lo + FFN_COLS, :], preferred_element_type=F32)
    o_ref[...] = x + 0.5 * _rmsnorm(acc, g_ref[norm_row + 1:norm_row + 2, :])
    for src, dst in zip(cast_src, cast_dst):
        dst[...] = src[...].astype(BF16)


def _ffn(x, l, norm_row, norm_g, w_up, w_down, to_cast, cast_layer):
    n = x.shape[0]
    steps = n // FFN_TILE
    jobs = [_cast_job(src, cast_layer, steps) for src in to_cast]
    tile = pl.BlockSpec((FFN_TILE, D_MODEL), lambda i: (i, 0))
    out = pl.pallas_call(
        functools.partial(_ffn_kernel, norm_row, len(jobs)),
        grid=(steps,),
        in_specs=[tile, _layer(norm_g, l), _resident(w_up.shape), _resident(w_down.shape)]
        + [j[0] for j in jobs],
        out_specs=[tile] + [j[1] for j in jobs],
        out_shape=[jax.ShapeDtypeStruct((n, D_MODEL), F32)] + [j[2] for j in jobs],
        compiler_params=pltpu.CompilerParams(
            dimension_semantics=("arbitrary",), vmem_limit_bytes=VMEM_LIMIT),
        name="ffn",
    )(x, norm_g, w_up, w_down, *to_cast)
    return out[0], out[1:]


def _scan_rows(a, b, h0):
    n, w = a.shape
    groups = n // SUBLANES
    a = a.reshape(groups, SUBLANES, w)
    b = b.reshape(groups, SUBLANES, w)
    sub = lax.broadcasted_iota(jnp.int32, a.shape, 1)
    d = 1
    while d < SUBLANES:
        keep = sub >= d
        a_prev = jnp.where(keep, pltpu.roll(a, d, 1), 1.0)
        b_prev = jnp.where(keep, pltpu.roll(b, d, 1), 0.0)
        b = a * b_prev + b
        a = a * a_prev
        d *= 2
    out = []
    for i in range(groups):
        hi = a[i] * h0 + b[i]
        out.append(hi)
        h0 = hi[SUBLANES - 1:SUBLANES, :]
    return jnp.concatenate(out, axis=0)


def _mixer_in_kernel(has_vmix, tiles_per_seq, *refs):
    (x_ref, g_ref, win_ref, mu_ref, w0_ref, wup_ref, a0_ref, aup_ref, gup_ref,
     kk_ref, ka_ref, rk_ref) = refs[:12]
    refs = refs[12:]
    if has_vmix:
        v0_ref, vdown_ref, vup_ref, vfirst_ref = refs[:4]
        refs = refs[4:]
    (convw_ref, convb_ref, wax_ref, bax_ref, lam_ref, hsum_ref,
     r_out, lw_out, k_out, v_out, kkn_out, a_out, bonus_out, g_out, ob_out,
     proj_even, proj_odd, pcarry, xcarry, hcarry) = refs

    tm = x_ref.shape[0]
    step = pl.program_id(0)
    first = (step + tiles_per_seq - 1) % tiles_per_seq == 0

    @pl.when(step == 0)
    def _():
        proj_odd[...] = jnp.zeros(proj_odd.shape, F32)

    @pl.when(jnp.logical_or(first, step == 0))
    def _():
        pcarry[...] = jnp.zeros(pcarry.shape, F32)
        xcarry[...] = jnp.zeros(xcarry.shape, F32)
        hcarry[...] = jnp.zeros(hcarry.shape, F32)

    def body(proj_new, proj):
        h = _rmsnorm(x_ref[...], g_ref[MIXER_NORM_ROW:MIXER_NORM_ROW + 1, :]).astype(BF16)
        chunk_starts = iter(range(0, W_MIX, PROJ_COLS))

        def project(n_chunks):
            for _ in range(n_chunks):
                lo = next(chunk_starts)
                proj_new[:, lo:lo + PROJ_COLS] = jnp.dot(
                    h, win_ref[:, lo:lo + PROJ_COLS], preferred_element_type=F32)

        project(PROJ_SPLIT[0])
        p = proj[:, :W_SHIFT]
        prev = _shift_rows(pcarry[...], p, 1)
        pcarry[...] = p[tm - SUBLANES:, :]
        p = p + mu_ref[...] * (prev - p)
        rw = RWKV_WIDTH
        r, k, v = p[:, 0:rw], p[:, rw:2 * rw], p[:, 2 * rw:3 * rw]
        xwa = p[:, 3 * rw:3 * rw + DECAY_LORA + AAA_LORA]
        xg = p[:, 3 * rw + DECAY_LORA + AAA_LORA:]
        w = -_softplus(-(w0_ref[...] + _dot(jnp.tanh(xwa), wup_ref[...])), accurate=False) - 0.5
        a = jax.nn.sigmoid(a0_ref[...] + _dot(xwa, aup_ref[...]))
        g_out[...] = _dot(jax.nn.sigmoid(xg), gup_ref[...])
        if has_vmix:
            v_lora = _dot(v, vdown_ref[...])
        project(PROJ_SPLIT[1])
        lw_out[...] = -jnp.exp(w)
        if has_vmix:
            mix = jax.nn.sigmoid(v0_ref[...] + _dot(v_lora, vup_ref[...]))
            v = v + (vfirst_ref[...] - v) * mix
        kk = k * kk_ref[...]
        kk = kk * lax.rsqrt(jnp.maximum(_dot(kk * kk, hsum_ref[...]), 1e-24))
        project(PROJ_SPLIT[2])
        k = k * (1.0 + (a - 1.0) * ka_ref[...])
        r_out[...] = r
        k_out[...] = k
        v_out[...] = v
        kkn_out[...] = kk
        a_out[...] = a
        bonus_out[...] = _dot(r * k * rk_ref[...], hsum_ref[...]) * v
        project(PROJ_SPLIT[3])

        lw_ = LRU_WIDTH
        lx = proj[:, W_SHIFT:W_SHIFT + lw_]
        lgate = proj[:, W_SHIFT + lw_:]
        xprev = xcarry[...]
        xb = convb_ref[...] + convw_ref[CONV_WIDTH - 1:CONV_WIDTH, :] * lx
        for d in range(1, CONV_WIDTH):
            xb = xb + convw_ref[CONV_WIDTH - 1 - d:CONV_WIDTH - d, :] * _shift_rows(xprev, lx, d)
        xcarry[...] = lx[tm - SUBLANES:, :]
        gates = _dot(xb, wax_ref[...]) + bax_ref[...]
        project(PROJ_SPLIT[4])
        gate_a = jax.nn.sigmoid(gates[:, :lw_])
        gate_x = jax.nn.sigmoid(gates[:, lw_:])
        log_a = -LRU_C * gate_a * _softplus(-lam_ref[...], accurate=True)
        a_l = jnp.exp(log_a)
        mult = jnp.sqrt(-jnp.tanh(log_a) * (a_l * a_l + 1.0))
        row = lax.broadcasted_iota(jnp.int32, log_a.shape, 0)
        seq_start = row + jnp.where(first, 0, 1) == 0
        mult = jnp.where(seq_start, 1.0, mult)
        hseq = _scan_rows(a_l, xb * gate_x * mult, hcarry[SUBLANES - 1:SUBLANES, :])
        hcarry[...] = hseq[tm - SUBLANES:, :]
        ob_out[...] = hseq * jax.nn.gelu(lgate)

    @pl.when(step % 2 == 0)
    def _():
        body(proj_even, proj_odd)

    @pl.when(step % 2 == 1)
    def _():
        body(proj_odd, proj_even)


def _mixer_in(x, seq, l, norm_g, w_in_b, prm, v_first, hsum):
    n = x.shape[0]
    tm = MIX_TILE
    tiles = n // tm
    has_vmix = v_first is not None
    x_spec = pl.BlockSpec((tm, D_MODEL), lambda i: (jnp.minimum(i, tiles - 1), 0))
    lag512 = pl.BlockSpec((tm, RWKV_WIDTH), lambda i: (jnp.maximum(i - 1, 0), 0))
    names = ["shift_mu", "w0", "wup_pad", "a0", "aup_pad", "g_up", "k_k", "k_a", "r_k"]
    w_mix_spec = pl.BlockSpec((D_MODEL, W_MIX), lambda i: (0, 0), pipeline_mode=pl.Buffered(1))
    in_specs = [x_spec, _layer(norm_g, l), w_mix_spec] + [_layer(prm[k], l) for k in names]
    args = [x, norm_g, w_in_b] + [prm[k] for k in names]
    if has_vmix:
        vnames = ["v0", "v_down", "v_up"]
        in_specs += [_layer(prm[k], l - 1) for k in vnames] + [lag512]
        args += [prm[k] for k in vnames] + [v_first]
    names = ["conv_w", "conv_b", "wax", "bax", "lru_lambda"]
    in_specs += [_layer(prm[k], l) for k in names] + [_full((RWKV_WIDTH, RWKV_WIDTH))]
    args += [prm[k] for k in names] + [hsum]
    out512 = jax.ShapeDtypeStruct((n, RWKV_WIDTH), F32)
    return pl.pallas_call(
        functools.partial(_mixer_in_kernel, has_vmix, seq // tm),
        grid=(tiles + 1,),
        in_specs=in_specs,
        out_specs=[lag512] * 9,
        out_shape=[out512] * 9,
        scratch_shapes=[
            pltpu.VMEM((tm, W_MIX), F32),
            pltpu.VMEM((tm, W_MIX), F32),
            pltpu.VMEM((SUBLANES, W_SHIFT), F32),
            pltpu.VMEM((SUBLANES, LRU_WIDTH), F32),
            pltpu.VMEM((SUBLANES, LRU_WIDTH), F32),
        ],
        compiler_params=pltpu.CompilerParams(
            dimension_semantics=("arbitrary",), vmem_limit_bytes=VMEM_LIMIT),
        name="mixer_in",
    )(*args)


def _cumsum_rows(x):
    n, w = x.shape
    groups = n // SUBLANES
    x = x.reshape(groups, SUBLANES, w)
    sub = lax.broadcasted_iota(jnp.int32, x.shape, 1)
    d = 1
    while d < SUBLANES:
        x = x + jnp.where(sub >= d, pltpu.roll(x, d, 1), 0.0)
        d *= 2
    out = [x[0]]
    for i in range(1, groups):
        out.append(x[i] + out[-1][SUBLANES - 1:SUBLANES, :])
    return jnp.concatenate(out, axis=0)


def _rwkv_kernel(r_ref, lw_ref, k_ref, v_ref, kk_ref, a_ref, y_ref, s_ref):
    c = CHUNK
    nb = r_ref.shape[0]
    pairs = RWKV_WIDTH // LANES
    chains = [(bi, p) for bi in range(nb) for p in range(pairs)]

    @pl.when(pl.program_id(0) == 0)
    def _():
        s_ref[...] = jnp.zeros(s_ref.shape, F32)

    lane = lax.broadcasted_iota(jnp.int32, (c, LANES), 1)
    t_idx = lax.broadcasted_iota(jnp.int32, (c, LANES), 0)
    left = lane < HEAD_DIM
    s_idx = jnp.where(left, lane, lane - HEAD_DIM)
    strict = t_idx > s_idx
    incl = t_idx >= s_idx
    eye_w = (t_idx == s_idx).astype(F32)
    row2 = lax.broadcasted_iota(jnp.int32, (LANES, LANES), 0)
    lane2 = lax.broadcasted_iota(jnp.int32, (LANES, LANES), 1)
    same_head = (row2 < HEAD_DIM) == (lane2 < HEAD_DIM)
    eye2 = row2 == lane2

    def bd(w):
        return jnp.concatenate([jnp.where(left, w, 0.0), jnp.where(left, 0.0, w)], axis=0)

    def prep(g):
        ops = {}
        rows = slice(g * c, (g + 1) * c)
        for bi in range(nb):
            lw = lw_ref[bi, rows, :]
            cum = _cumsum_rows(lw)
            last = cum[c - 1:c, :]
            kk = kk_ref[bi, rows, :]
            k = k_ref[bi, rows, :]
            b = kk * a_ref[bi, rows, :]
            e_neg = jnp.exp(-cum)
            e_tail = jnp.exp(last - cum)
            a_t = -kk * jnp.exp(cum - lw)
            r_t = r_ref[bi, rows, :] * jnp.exp(cum)
            b_t = b * e_neg
            k_t = k * e_neg
            b_h = b * e_tail
            k_h = k * e_tail
            w_c = jnp.exp(last)
            v = v_ref[bi, rows, :]
            for p in range(pairs):
                sl = slice(p * LANES, (p + 1) * LANES)
                w_col = jnp.sum(jnp.where(eye2, w_c[:, sl], 0.0), axis=1, keepdims=True)
                ops[bi, p] = dict(a=a_t[:, sl], r=r_t[:, sl], bt=b_t[:, sl], kt=k_t[:, sl],
                                  bh=b_h[:, sl], kh=k_h[:, sl], v=v[:, sl], w_col=w_col)
        for ch in chains:
            o = ops[ch]
            m = _dot_nt(jnp.concatenate([o["a"], o["r"]], axis=0),
                        jnp.concatenate([bd(o["bt"]), bd(o["kt"])], axis=0))
            o["a_ab"] = jnp.where(strict, m[:c, :LANES], 0.0)
            o["a_ak"] = jnp.where(strict, m[:c, LANES:], 0.0)
            o["a_rb"] = jnp.where(incl, m[c:, :LANES], 0.0)
            o["a_rk"] = jnp.where(incl, m[c:, LANES:], 0.0)
        return ops

    def inverse_stages(ops):
        def first():
            for ch in chains:
                o = ops[ch]
                o["t"] = eye_w + o["a_ab"]
                o["p"] = _dot(o["a_ab"], bd(o["a_ab"]))

        def double(is_last):
            def run():
                for ch in chains:
                    o = ops[ch]
                    p_bd = bd(o["p"])
                    if is_last:
                        o["t"] = o["t"] + _dot(o["t"], p_bd)
                    else:
                        prod = _dot(jnp.concatenate([o["t"], o["p"]], axis=0), p_bd)
                        o["t"] = o["t"] + prod[:c]
                        o["p"] = prod[c:]
            return run

        steps = CHUNK.bit_length() - 2
        return [first] + [double(i == steps - 1) for i in range(steps)]

    def state_stages(ops, state, g):
        rows = slice(g * c, (g + 1) * c)

        def stage_x():
            for ch in chains:
                o = ops[ch]
                o["v_bd"] = bd(o["v"])
                o["x"] = _dot(jnp.concatenate([o["a"], o["a_ak"]], axis=1),
                              jnp.concatenate([state[ch], o["v_bd"]], axis=0))

        def stage_u():
            for ch in chains:
                o = ops[ch]
                o["u"] = _dot(o["t"], bd(o["x"]))

        def stage_s():
            for ch in chains:
                o = ops[ch]
                ds = _dot_tn(jnp.concatenate([o["bh"], o["kh"]], axis=0),
                             jnp.concatenate([o["u"], o["v"]], axis=0))
                o["s_new"] = o["w_col"] * state[ch] + jnp.where(same_head, ds, 0.0)

        def stage_y():
            for ch in chains:
                o = ops[ch]
                bi, p = ch
                y_ref[bi, rows, p * LANES:(p + 1) * LANES] = _dot(
                    jnp.concatenate([o["r"], o["a_rb"], o["a_rk"]], axis=1),
                    jnp.concatenate([state[ch], bd(o["u"]), o["v_bd"]], axis=0))
                state[ch] = o["s_new"]

        return [stage_x, stage_u, stage_s, stage_y]

    state = {ch: s_ref[ch[0], ch[1]] for ch in chains}
    ops_next = prep(0)
    pending = []
    for g in range(RWKV_CHUNKS):
        ops_cur = ops_next
        inv = inverse_stages(ops_cur)
        half = len(inv) // 2
        for i, stage in enumerate(inv):
            stage()
            if i < len(pending):
                pending[i]()
            if i == half and g + 1 < RWKV_CHUNKS:
                ops_next = prep(g + 1)
        for stage in pending[len(inv):]:
            stage()
        pending = state_stages(ops_cur, state, g)
    for stage in pending:
        stage()
    for ch, s_val in state.items():
        s_ref[ch[0], ch[1]] = s_val


def _rwkv(r, lw, k, v, kk, a, batch, seq):
    rows = RWKV_CHUNKS * CHUNK
    spec = pl.BlockSpec((batch, rows, RWKV_WIDTH), lambda i: (0, i, 0))
    shape3 = (batch, seq, RWKV_WIDTH)
    y = pl.pallas_call(
        _rwkv_kernel,
        grid=(seq // rows,),
        in_specs=[spec] * 6,
        out_specs=spec,
        out_shape=jax.ShapeDtypeStruct(shape3, F32),
        scratch_shapes=[pltpu.VMEM((batch, RWKV_WIDTH // LANES, LANES, LANES), F32)],
        compiler_params=pltpu.CompilerParams(
            dimension_semantics=("arbitrary",), vmem_limit_bytes=VMEM_LIMIT),
        name="rwkv",
    )(*(t.reshape(shape3) for t in (r, lw, k, v, kk, a)))
    return y.reshape(batch * seq, RWKV_WIDTH)


def _mixer_out_kernel(x_ref, y_ref, bonus_ref, g_ref, ob_ref, gnorm_ref, win_ref, prw_ref, plr_ref,
                      wout_ref, lnw_ref, lnb_ref, hmean_ref, o_ref):
    x = x_ref[...]
    pre = gnorm_ref[MIXER_NORM_ROW:MIXER_NORM_ROW + 1, :]
    gates = jax.nn.sigmoid(_dot(_rmsnorm(x, pre), win_ref[:, W_MIX:]))
    y = y_ref[...]
    mu = _dot(y, hmean_ref[...])
    yc = y - mu
    var = _dot(yc * yc, hmean_ref[...])
    yn = yc * lax.rsqrt(var + GN_EPS) * lnw_ref[...] + lnb_ref[...] + bonus_ref[...]
    o_a = yn * g_ref[...]
    merged = (gates[:, :D_MODEL] * _dot(o_a, prw_ref[...])
              + gates[:, D_MODEL:] * _dot(ob_ref[...], plr_ref[...]))
    out = _dot(merged, wout_ref[...])
    o_ref[...] = x + _rmsnorm(out, gnorm_ref[MIXER_NORM_ROW + 1:MIXER_NORM_ROW + 2, :])


def _mixer_out(x, y, bonus, g, o_b, l, norm_g, weights, prm, hmean):
    n = x.shape[0]
    tm = MIX_OUT_TILE
    row512 = pl.BlockSpec((tm, RWKV_WIDTH), lambda i: (i, 0))
    row1024 = pl.BlockSpec((tm, D_MODEL), lambda i: (i, 0))
    names = ["ln_w", "ln_b"]
    return pl.pallas_call(
        _mixer_out_kernel,
        grid=(n // tm,),
        in_specs=([row1024, row512, row512, row512, row512, _layer(norm_g, l)]
                  + [_resident(w.shape) for w in weights]
                  + [_layer(prm[k], l) for k in names] + [_full((RWKV_WIDTH, RWKV_WIDTH))]),
        out_specs=row1024,
        out_shape=jax.ShapeDtypeStruct((n, D_MODEL), F32),
        compiler_params=pltpu.CompilerParams(
            dimension_semantics=("arbitrary",), vmem_limit_bytes=VMEM_LIMIT),
        name="mixer_out",
    )(x, y, bonus, g, o_b, norm_g, *weights, *[prm[k] for k in names], hmean)


def _block_diag(w):
    nl, nb, d, e = w.shape
    eye = jnp.eye(nb, dtype=w.dtype)
    return jnp.einsum("lnde,nm->lndme", w, eye).reshape(nl, nb * d, nb * e)


def kernel(x, norm_g, ffn1_w_up, ffn1_w_down, ffn2_w_up, ffn2_w_down, w_in, shift_mu, w0, w_up, a0, a_up, g_up, k_k, k_a, r_k, ln_w, ln_b, v0, v_down, v_up, conv_w, conv_b, lru_wa, lru_ba, lru_wx, lru_bx, lru_lambda, p_rwkv, p_lru, w_out):
    batch, seq, d = x.shape
    depth = norm_g.shape[0]
    assert d == D_MODEL and seq % MIX_TILE == 0 and (batch * seq) % FFN_TILE == 0
    assert seq % (RWKV_CHUNKS * CHUNK) == 0 and (batch * seq) % MIX_OUT_TILE == 0
    xf = x.reshape(batch * seq, d)
    bf = lambda t: t.astype(BF16)
    rows = lambda t: t.reshape(t.shape[0], 1, -1)
    head = jnp.arange(RWKV_WIDTH) // HEAD_DIM
    same = (head[:, None] == head[None, :])
    hsum = same.astype(BF16)
    hmean = (same.astype(F32) / HEAD_DIM).astype(BF16)
    zeros_lora = jnp.zeros((depth, DECAY_LORA, RWKV_WIDTH), F32)
    prm = dict(
        shift_mu=rows(shift_mu), w0=rows(w0), a0=rows(a0), k_k=rows(k_k), k_a=rows(k_a), r_k=rows(r_k),
        wup_pad=bf(jnp.concatenate([w_up, zeros_lora], axis=1)),
        aup_pad=bf(jnp.concatenate([zeros_lora, a_up], axis=1)),
        g_up=bf(g_up), v0=rows(v0), v_down=bf(v_down), v_up=bf(v_up),
        conv_w=conv_w, conv_b=rows(conv_b),
        wax=bf(jnp.concatenate([_block_diag(lru_wa), _block_diag(lru_wx)], axis=2)),
        bax=rows(jnp.concatenate([lru_ba, lru_bx], axis=1)), lru_lambda=rows(lru_lambda),
        ln_w=rows(ln_w), ln_b=rows(ln_b),
    )
    ffn1_b = (bf(ffn1_w_up[0]), bf(ffn1_w_down[0]))
    v_first = None
    for l in range(depth):
        xf, (w_in_b, p_rwkv_b, p_lru_b, w_out_b, *ffn2_b) = _ffn(
            xf, l, FFN1_NORM_ROW, norm_g, *ffn1_b,
            to_cast=[w_in, p_rwkv, p_lru, w_out, ffn2_w_up, ffn2_w_down], cast_layer=l)
        r, lw, k, v, kk, a, bonus, gate, o_b = _mixer_in(xf, seq, l, norm_g, w_in_b, prm, v_first, hsum)
        if l == 0:
            v_first = v
        y = _rwkv(r, lw, k, v, kk, a, batch, seq)
        xf = _mixer_out(xf, y, bonus, gate, o_b, l, norm_g, (w_in_b, p_rwkv_b, p_lru_b, w_out_b), prm, hmean)
        nxt = [ffn1_w_up, ffn1_w_down] if l + 1 < depth else []
        xf, ffn1_b = _ffn(xf, l, FFN2_NORM_ROW, norm_g, *ffn2_b, to_cast=nxt, cast_layer=l + 1)
    return xf.reshape(batch, seq, d)
```

```python
import functools

import jax
import jax.numpy as jnp
from jax import lax
from jax.experimental import pallas as pl
from jax.experimental.pallas import tpu as pltpu

D_MODEL = 1024
RWKV_WIDTH = 512
HEAD_DIM = 64
LRU_WIDTH = 512
LRU_BLOCKS = 8
CONV_WIDTH = 4
LRU_C = 8.0
D_FF = 2816
RMS_EPS = 1e-6
GN_EPS = 64e-5
FFN1_NORM_ROW = 0
MIXER_NORM_ROW = 2
FFN2_NORM_ROW = 4
DECAY_LORA = 64
AAA_LORA = 64
GATE_LORA = 128
W_SHIFT = 3 * RWKV_WIDTH + DECAY_LORA + AAA_LORA + GATE_LORA
W_MIX = W_SHIFT + 2 * LRU_WIDTH
W_IN = W_MIX + 2 * D_MODEL

RWKV_CHUNKS = 8
CHUNK = 64
LANES = 128
SUBLANES = 8
BF16_ROWS = 16
FFN_TILE = 512
FFN_COLS = 256
MIX_TILE = 256
MIX_OUT_TILE = 512
PROJ_COLS = 256
PROJ_SPLIT = (2, 2, 2, 5)
VMEM_LIMIT = 56 * 1024 * 1024

BF16 = jnp.bfloat16
F32 = jnp.float32


def _dot(a, b):
    return jnp.dot(a.astype(BF16), b.astype(BF16), preferred_element_type=F32)


def _dot_nt(a, b):
    return lax.dot_general(a.astype(BF16), b.astype(BF16), (((1,), (1,)), ((), ())),
                           preferred_element_type=F32)


def _dot_tn(a, b):
    return lax.dot_general(a.astype(BF16), b.astype(BF16), (((0,), (0,)), ((), ())),
                           preferred_element_type=F32)


def _rmsnorm(x, g):
    ms = jnp.mean(x * x, axis=-1, keepdims=True)
    return x * lax.rsqrt(ms + RMS_EPS) * g


def _softplus(x, accurate):
    e = jnp.exp(-jnp.abs(x))
    return jnp.maximum(x, 0.0) + (jnp.log1p(e) if accurate else jnp.log(1.0 + e))


def _shift_rows(prev_rows, x, d):
    ext = jnp.concatenate([prev_rows, x], axis=0)
    return pltpu.roll(ext, d, 0)[SUBLANES:]


def _full(shape):
    return pl.BlockSpec(shape, lambda *_: (0,) * len(shape))


def _resident(shape):
    return pl.BlockSpec(shape, lambda *_: (0,) * len(shape), pipeline_mode=pl.Buffered(1))


def _layer(arr, l):
    shape = arr.shape[1:]
    return pl.BlockSpec((None,) + shape, lambda *_: (l,) + (0,) * len(shape),
                        pipeline_mode=pl.Buffered(1))


def _cast_job(src, l, steps):
    _, rows, cols = src.shape
    blk = next(b for b in range(BF16_ROWS, rows + 1, BF16_ROWS)
               if rows % b == 0 and b * steps >= rows)
    last = rows // blk - 1
    in_spec = pl.BlockSpec((None, blk, cols), lambda i: (l, jnp.minimum(i, last), 0))
    out_spec = pl.BlockSpec((blk, cols), lambda i: (jnp.minimum(i, last), 0))
    return in_spec, out_spec, jax.ShapeDtypeStruct((rows, cols), BF16)


def _ffn_kernel(norm_row, n_cast, x_ref, g_ref, wup_ref, wdn_ref, *refs):
    cast_src, o_ref, cast_dst = refs[:n_cast], refs[n_cast], refs[n_cast + 1:]
    x = x_ref[...]
    h = _rmsnorm(x, g_ref[norm_row:norm_row + 1, :]).astype(BF16)
    acc = jnp.zeros(x.shape, F32)
    for j in range(D_FF // FFN_COLS):
        lo = j * FFN_COLS
        gate = jnp.dot(h, wup_ref[:, lo:lo + FFN_COLS], preferred_element_type=F32)
        up = jnp.dot(h, wup_ref[:, D_FF + lo:D_FF + lo + FFN_COLS], preferred_element_type=F32)
        act = (gate * jax.nn.sigmoid(gate) * up).astype(BF16)
        acc = acc + jnp.dot(act, wdn_ref[lo:lo + FFN_COLS, :], preferred_element_type=F32)
    o_ref[...] = x + 0.5 * _rmsnorm(acc, g_ref[norm_row + 1:norm_row + 2, :])
    for src, dst in zip(cast_src, cast_dst):
        dst[...] = src[...].astype(BF16)


def _ffn(x, l, norm_row, norm_g, w_up, w_down, to_cast, cast_layer):
    n = x.shape[0]
    steps = n // FFN_TILE
    jobs = [_cast_job(src, cast_layer, steps) for src in to_cast]
    tile = pl.BlockSpec((FFN_TILE, D_MODEL), lambda i: (i, 0))
    out = pl.pallas_call(
        functools.partial(_ffn_kernel, norm_row, len(jobs)),
        grid=(steps,),
        in_specs=[tile, _layer(norm_g, l), _resident(w_up.shape), _resident(w_down.shape)]
        + [j[0] for j in jobs],
        out_specs=[tile] + [j[1] for j in jobs],
        out_shape=[jax.ShapeDtypeStruct((n, D_MODEL), F32)] + [j[2] for j in jobs],
        compiler_params=pltpu.CompilerParams(
            dimension_semantics=("arbitrary",), vmem_limit_bytes=VMEM_LIMIT),
        name="ffn",
    )(x, norm_g, w_up, w_down, *to_cast)
    return out[0], out[1:]


def _scan_rows(a, b, h0):
    n, w = a.shape
    groups = n // SUBLANES
    a = a.reshape(groups, SUBLANES, w)
    b = b.reshape(groups, SUBLANES, w)
    sub = lax.broadcasted_iota(jnp.int32, a.shape, 1)
    d = 1
    while d < SUBLANES:
        keep = sub >= d
        a_prev = jnp.where(keep, pltpu.roll(a, d, 1), 1.0)
        b_prev = jnp.where(keep, pltpu.roll(b, d, 1), 0.0)
        b = a * b_prev + b
        a = a * a_prev
        d *= 2
    out = []
    for i in range(groups):
        hi = a[i] * h0 + b[i]
        out.append(hi)
        h0 = hi[SUBLANES - 1:SUBLANES, :]
    return jnp.concatenate(out, axis=0)


def _mixer_in_kernel(has_vmix, tiles_per_seq, *refs):
    (x_ref, g_ref, win_ref, mu_ref, w0_ref, wup_ref, a0_ref, aup_ref, gup_ref,
     kk_ref, ka_ref, rk_ref) = refs[:12]
    refs = refs[12:]
    if has_vmix:
        v0_ref, vdown_ref, vup_ref, vfirst_ref = refs[:4]
        refs = refs[4:]
    (hsum_ref, r_out, lw_out, k_out, v_out, kkn_out, a_out, bonus_out, g_out, lru_out,
     proj_even, proj_odd, pcarry) = refs

    tm = x_ref.shape[0]
    step = pl.program_id(0)
    first = (step + tiles_per_seq - 1) % tiles_per_seq == 0

    @pl.when(step == 0)
    def _():
        proj_odd[...] = jnp.zeros(proj_odd.shape, F32)

    @pl.when(jnp.logical_or(first, step == 0))
    def _():
        pcarry[...] = jnp.zeros(pcarry.shape, F32)

    def body(proj_new, proj):
        h = _rmsnorm(x_ref[...], g_ref[MIXER_NORM_ROW:MIXER_NORM_ROW + 1, :]).astype(BF16)
        chunk_starts = iter(range(0, W_MIX, PROJ_COLS))

        def project(n_chunks):
            for _ in range(n_chunks):
                lo = next(chunk_starts)
                proj_new[:, lo:lo + PROJ_COLS] = jnp.dot(
                    h, win_ref[:, lo:lo + PROJ_COLS], preferred_element_type=F32)

        project(PROJ_SPLIT[0])
        p = proj[:, :W_SHIFT]
        prev = _shift_rows(pcarry[...], p, 1)
        pcarry[...] = p[tm - SUBLANES:, :]
        p = p + mu_ref[...] * (prev - p)
        rw = RWKV_WIDTH
        r, k, v = p[:, 0:rw], p[:, rw:2 * rw], p[:, 2 * rw:3 * rw]
        xwa = p[:, 3 * rw:3 * rw + DECAY_LORA + AAA_LORA]
        xg = p[:, 3 * rw + DECAY_LORA + AAA_LORA:]
        w = -_softplus(-(w0_ref[...] + _dot(jnp.tanh(xwa), wup_ref[...])), accurate=False) - 0.5
        a = jax.nn.sigmoid(a0_ref[...] + _dot(xwa, aup_ref[...]))
        g_out[...] = _dot(jax.nn.sigmoid(xg), gup_ref[...])
        if has_vmix:
            v_lora = _dot(v, vdown_ref[...])
        project(PROJ_SPLIT[1])
        lw_out[...] = -jnp.exp(w)
        if has_vmix:
            mix = jax.nn.sigmoid(v0_ref[...] + _dot(v_lora, vup_ref[...]))
            v = v + (vfirst_ref[...] - v) * mix
        kk = k * kk_ref[...]
        kk = kk * lax.rsqrt(jnp.maximum(_dot(kk * kk, hsum_ref[...]), 1e-24))
        project(PROJ_SPLIT[2])
        k = k * (1.0 + (a - 1.0) * ka_ref[...])
        r_out[...] = r
        k_out[...] = k
        v_out[...] = v
        kkn_out[...] = kk
        a_out[...] = a
        bonus_out[...] = _dot(r * k * rk_ref[...], hsum_ref[...]) * v
        project(PROJ_SPLIT[3])
        lru_out[...] = proj[:, W_SHIFT:]

    @pl.when(step % 2 == 0)
    def _():
        body(proj_even, proj_odd)

    @pl.when(step % 2 == 1)
    def _():
        body(proj_odd, proj_even)


def _mixer_in(x, seq, l, norm_g, w_in_b, prm, v_first, hsum):
    n = x.shape[0]
    tm = MIX_TILE
    tiles = n // tm
    has_vmix = v_first is not None
    x_spec = pl.BlockSpec((tm, D_MODEL), lambda i: (jnp.minimum(i, tiles - 1), 0))
    lag512 = pl.BlockSpec((tm, RWKV_WIDTH), lambda i: (jnp.maximum(i - 1, 0), 0))
    names = ["shift_mu", "w0", "wup_pad", "a0", "aup_pad", "g_up", "k_k", "k_a", "r_k"]
    w_mix_spec = pl.BlockSpec((D_MODEL, W_MIX), lambda i: (0, 0), pipeline_mode=pl.Buffered(1))
    in_specs = [x_spec, _layer(norm_g, l), w_mix_spec] + [_layer(prm[k], l) for k in names]
    args = [x, norm_g, w_in_b] + [prm[k] for k in names]
    if has_vmix:
        vnames = ["v0", "v_down", "v_up"]
        in_specs += [_layer(prm[k], l - 1) for k in vnames] + [lag512]
        args += [prm[k] for k in vnames] + [v_first]
    in_specs += [_full((RWKV_WIDTH, RWKV_WIDTH))]
    args += [hsum]
    out512 = jax.ShapeDtypeStruct((n, RWKV_WIDTH), F32)
    lag_lru = pl.BlockSpec((tm, 2 * LRU_WIDTH), lambda i: (jnp.maximum(i - 1, 0), 0))
    return pl.pallas_call(
        functools.partial(_mixer_in_kernel, has_vmix, seq // tm),
        grid=(tiles + 1,),
        in_specs=in_specs,
        out_specs=[lag512] * 8 + [lag_lru],
        out_shape=[out512] * 8 + [jax.ShapeDtypeStruct((n, 2 * LRU_WIDTH), F32)],
        scratch_shapes=[
            pltpu.VMEM((tm, W_MIX), F32),
            pltpu.VMEM((tm, W_MIX), F32),
            pltpu.VMEM((SUBLANES, W_SHIFT), F32),
        ],
        compiler_params=pltpu.CompilerParams(
            dimension_semantics=("arbitrary",), vmem_limit_bytes=VMEM_LIMIT),
        name="mixer_in",
    )(*args)


def _cumsum_rows(x):
    n, w = x.shape
    groups = n // SUBLANES
    x = x.reshape(groups, SUBLANES, w)
    sub = lax.broadcasted_iota(jnp.int32, x.shape, 1)
    d = 1
    while d < SUBLANES:
        x = x + jnp.where(sub >= d, pltpu.roll(x, d, 1), 0.0)
        d *= 2
    out = [x[0]]
    for i in range(1, groups):
        out.append(x[i] + out[-1][SUBLANES - 1:SUBLANES, :])
    return jnp.concatenate(out, axis=0)


def _rwkv_kernel(r_ref, lw_ref, k_ref, v_ref, kk_ref, a_ref, y_ref, s_ref):
    c = CHUNK
    nb = r_ref.shape[0]
    pairs = RWKV_WIDTH // LANES
    chains = [(bi, p) for bi in range(nb) for p in range(pairs)]

    @pl.when(pl.program_id(0) == 0)
    def _():
        s_ref[...] = jnp.zeros(s_ref.shape, F32)

    lane = lax.broadcasted_iota(jnp.int32, (c, LANES), 1)
    t_idx = lax.broadcasted_iota(jnp.int32, (c, LANES), 0)
    left = lane < HEAD_DIM
    s_idx = jnp.where(left, lane, lane - HEAD_DIM)
    strict = t_idx > s_idx
    incl = t_idx >= s_idx
    eye_w = (t_idx == s_idx).astype(F32)
    row2 = lax.broadcasted_iota(jnp.int32, (LANES, LANES), 0)
    lane2 = lax.broadcasted_iota(jnp.int32, (LANES, LANES), 1)
    same_head = (row2 < HEAD_DIM) == (lane2 < HEAD_DIM)
    eye2 = row2 == lane2

    def bd(w):
        return jnp.concatenate([jnp.where(left, w, 0.0), jnp.where(left, 0.0, w)], axis=0)

    def prep(g):
        ops = {}
        rows = slice(g * c, (g + 1) * c)
        for bi in range(nb):
            lw = lw_ref[bi, rows, :]
            cum = _cumsum_rows(lw)
            last = cum[c - 1:c, :]
            kk = kk_ref[bi, rows, :]
            k = k_ref[bi, rows, :]
            b = kk * a_ref[bi, rows, :]
            e_neg = jnp.exp(-cum)
            e_tail = jnp.exp(last - cum)
            a_t = -kk * jnp.exp(cum - lw)
            r_t = r_ref[bi, rows, :] * jnp.exp(cum)
            b_t = b * e_neg
            k_t = k * e_neg
            b_h = b * e_tail
            k_h = k * e_tail
            w_c = jnp.exp(last)
            v = v_ref[bi, rows, :]
            for p in range(pairs):
                sl = slice(p * LANES, (p + 1) * LANES)
                w_col = jnp.sum(jnp.where(eye2, w_c[:, sl], 0.0), axis=1, keepdims=True)
                ops[bi, p] = dict(a=a_t[:, sl], r=r_t[:, sl], bt=b_t[:, sl], kt=k_t[:, sl],
                                  bh=b_h[:, sl], kh=k_h[:, sl], v=v[:, sl], w_col=w_col)
        for ch in chains:
            o = ops[ch]
            m = _dot_nt(jnp.concatenate([o["a"], o["r"]], axis=0),
                        jnp.concatenate([bd(o["bt"]), bd(o["kt"])], axis=0))
            o["a_ab"] = jnp.where(strict, m[:c, :LANES], 0.0)
            o["a_ak"] = jnp.where(strict, m[:c, LANES:], 0.0)
            o["a_rb"] = jnp.where(incl, m[c:, :LANES], 0.0)
            o["a_rk"] = jnp.where(incl, m[c:, LANES:], 0.0)
        return ops

    def inverse_stages(ops):
        def first():
            for ch in chains:
                o = ops[ch]
                o["t"] = eye_w + o["a_ab"]
                o["p"] = _dot(o["a_ab"], bd(o["a_ab"]))

        def double(is_last):
            def run():
                for ch in chains:
                    o = ops[ch]
                    p_bd = bd(o["p"])
                    if is_last:
                        o["t"] = o["t"] + _dot(o["t"], p_bd)
                    else:
                        prod = _dot(jnp.concatenate([o["t"], o["p"]], axis=0), p_bd)
                        o["t"] = o["t"] + prod[:c]
                        o["p"] = prod[c:]
            return run

        steps = CHUNK.bit_length() - 2
        return [first] + [double(i == steps - 1) for i in range(steps)]

    def state_stages(ops, state, g):
        rows = slice(g * c, (g + 1) * c)

        def stage_x():
            for ch in chains:
                o = ops[ch]
                o["v_bd"] = bd(o["v"])
                o["x"] = _dot(jnp.concatenate([o["a"], o["a_ak"]], axis=1),
                              jnp.concatenate([state[ch], o["v_bd"]], axis=0))

        def stage_u():
            for ch in chains:
                o = ops[ch]
                o["u"] = _dot(o["t"], bd(o["x"]))

        def stage_s():
            for ch in chains:
                o = ops[ch]
                ds = _dot_tn(jnp.concatenate([o["bh"], o["kh"]], axis=0),
                             jnp.concatenate([o["u"], o["v"]], axis=0))
                o["s_new"] = o["w_col"] * state[ch] + jnp.where(same_head, ds, 0.0)

        def stage_y():
            for ch in chains:
                o = ops[ch]
                bi, p = ch
                y_ref[bi, rows, p * LANES:(p + 1) * LANES] = _dot(
                    jnp.concatenate([o["r"], o["a_rb"], o["a_rk"]], axis=1),
                    jnp.concatenate([state[ch], bd(o["u"]), o["v_bd"]], axis=0))
                state[ch] = o["s_new"]

        return [stage_x, stage_u, stage_s, stage_y]

    state = {ch: s_ref[ch[0], ch[1]] for ch in chains}
    ops_next = prep(0)
    pending = []
    for g in range(RWKV_CHUNKS):
        ops_cur = ops_next
        inv = inverse_stages(ops_cur)
        half = len(inv) // 2
        for i, stage in enumerate(inv):
            stage()
            if i < len(pending):
                pending[i]()
            if i == half and g + 1 < RWKV_CHUNKS:
                ops_next = prep(g + 1)
        for stage in pending[len(inv):]:
            stage()
        pending = state_stages(ops_cur, state, g)
    for stage in pending:
        stage()
    for ch, s_val in state.items():
        s_ref[ch[0], ch[1]] = s_val


def _rwkv(r, lw, k, v, kk, a, batch, seq):
    rows = RWKV_CHUNKS * CHUNK
    spec = pl.BlockSpec((batch, rows, RWKV_WIDTH), lambda i: (0, i, 0))
    shape3 = (batch, seq, RWKV_WIDTH)
    y = pl.pallas_call(
        _rwkv_kernel,
        grid=(seq // rows,),
        in_specs=[spec] * 6,
        out_specs=spec,
        out_shape=jax.ShapeDtypeStruct(shape3, F32),
        scratch_shapes=[pltpu.VMEM((batch, RWKV_WIDTH // LANES, LANES, LANES), F32)],
        compiler_params=pltpu.CompilerParams(
            dimension_semantics=("arbitrary",), vmem_limit_bytes=VMEM_LIMIT),
        name="rwkv",
    )(*(t.reshape(shape3) for t in (r, lw, k, v, kk, a)))
    return y.reshape(batch * seq, RWKV_WIDTH)


def _mixer_out_kernel(tiles_per_seq, x_ref, y_ref, bonus_ref, g_ref, lru_ref, gnorm_ref, win_ref, prw_ref,
                      plr_ref, wout_ref, lnw_ref, lnb_ref, convw_ref, convb_ref, wax_ref, bax_ref, lam_ref,
                      hmean_ref, o_ref, xcarry, hcarry):
    tm = x_ref.shape[0]
    first = pl.program_id(0) % tiles_per_seq == 0

    @pl.when(first)
    def _():
        xcarry[...] = jnp.zeros(xcarry.shape, F32)
        hcarry[...] = jnp.zeros(hcarry.shape, F32)

    x = x_ref[...]
    hx = _rmsnorm(x, gnorm_ref[MIXER_NORM_ROW:MIXER_NORM_ROW + 1, :]).astype(BF16)

    y = y_ref[...]
    mu = _dot(y, hmean_ref[...])

    lw_ = LRU_WIDTH
    lx = lru_ref[:, :lw_]
    xprev = xcarry[...]
    xb = convb_ref[...] + convw_ref[CONV_WIDTH - 1:CONV_WIDTH, :] * lx
    for d in range(1, CONV_WIDTH):
        xb = xb + convw_ref[CONV_WIDTH - 1 - d:CONV_WIDTH - d, :] * _shift_rows(xprev, lx, d)
    xcarry[...] = lx[tm - SUBLANES:, :]
    lgates = _dot(xb, wax_ref[...]) + bax_ref[...]

    yc = y - mu
    var = _dot(yc * yc, hmean_ref[...])
    yn = yc * lax.rsqrt(var + GN_EPS) * lnw_ref[...] + lnb_ref[...] + bonus_ref[...]
    rwkv_proj = _dot(yn * g_ref[...], prw_ref[...])
    gate_r = jax.nn.sigmoid(jnp.dot(hx, win_ref[:, W_MIX:W_MIX + D_MODEL], preferred_element_type=F32))
    gate_l = jax.nn.sigmoid(jnp.dot(hx, win_ref[:, W_MIX + D_MODEL:], preferred_element_type=F32))

    gate_a = jax.nn.sigmoid(lgates[:, :lw_])
    gate_x = jax.nn.sigmoid(lgates[:, lw_:])
    log_a = -LRU_C * gate_a * _softplus(-lam_ref[...], accurate=True)
    a_l = jnp.exp(log_a)
    mult = jnp.sqrt(-jnp.tanh(log_a) * (a_l * a_l + 1.0))
    row = lax.broadcasted_iota(jnp.int32, log_a.shape, 0)
    seq_start = row + jnp.where(first, 0, 1) == 0
    mult = jnp.where(seq_start, 1.0, mult)
    hseq = _scan_rows(a_l, xb * gate_x * mult, hcarry[SUBLANES - 1:SUBLANES, :])
    hcarry[...] = hseq[tm - SUBLANES:, :]
    lru_proj = _dot(hseq * jax.nn.gelu(lru_ref[:, lw_:]), plr_ref[...])

    merged = gate_r * rwkv_proj + gate_l * lru_proj
    out = _dot(merged, wout_ref[...])
    o_ref[...] = x + _rmsnorm(out, gnorm_ref[MIXER_NORM_ROW + 1:MIXER_NORM_ROW + 2, :])


def _mixer_out(x, seq, y, bonus, g, lru_in, l, norm_g, weights, prm, hmean):
    n = x.shape[0]
    tm = MIX_OUT_TILE
    row512 = pl.BlockSpec((tm, RWKV_WIDTH), lambda i: (i, 0))
    row1024 = pl.BlockSpec((tm, D_MODEL), lambda i: (i, 0))
    names = ["ln_w", "ln_b", "conv_w", "conv_b", "wax", "bax", "lru_lambda"]
    return pl.pallas_call(
        functools.partial(_mixer_out_kernel, seq // tm),
        grid=(n // tm,),
        in_specs=([row1024, row512, row512, row512, row1024, _layer(norm_g, l)]
                  + [_resident(w.shape) for w in weights]
                  + [_layer(prm[k], l) for k in names] + [_full((RWKV_WIDTH, RWKV_WIDTH))]),
        out_specs=row1024,
        out_shape=jax.ShapeDtypeStruct((n, D_MODEL), F32),
        scratch_shapes=[
            pltpu.VMEM((SUBLANES, LRU_WIDTH), F32),
            pltpu.VMEM((SUBLANES, LRU_WIDTH), F32),
        ],
        compiler_params=pltpu.CompilerParams(
            dimension_semantics=("arbitrary",), vmem_limit_bytes=VMEM_LIMIT),
        name="mixer_out",
    )(x, y, bonus, g, lru_in, norm_g, *weights, *[prm[k] for k in names], hmean)


def _block_diag(w):
    nl, nb, d, e = w.shape
    eye = jnp.eye(nb, dtype=w.dtype)
    return jnp.einsum("lnde,nm->lndme", w, eye).reshape(nl, nb * d, nb * e)


def kernel(x, norm_g, ffn1_w_up, ffn1_w_down, ffn2_w_up, ffn2_w_down, w_in, shift_mu, w0, w_up, a0, a_up, g_up, k_k, k_a, r_k, ln_w, ln_b, v0, v_down, v_up, conv_w, conv_b, lru_wa, lru_ba, lru_wx, lru_bx, lru_lambda, p_rwkv, p_lru, w_out):
    batch, seq, d = x.shape
    depth = norm_g.shape[0]
    assert d == D_MODEL and seq % MIX_TILE == 0 and (batch * seq) % FFN_TILE == 0
    assert seq % (RWKV_CHUNKS * CHUNK) == 0 and seq % MIX_OUT_TILE == 0
    xf = x.reshape(batch * seq, d)
    bf = lambda t: t.astype(BF16)
    rows = lambda t: t.reshape(t.shape[0], 1, -1)
    head = jnp.arange(RWKV_WIDTH) // HEAD_DIM
    same = (head[:, None] == head[None, :])
    hsum = same.astype(BF16)
    hmean = (same.astype(F32) / HEAD_DIM).astype(BF16)
    zeros_lora = jnp.zeros((depth, DECAY_LORA, RWKV_WIDTH), F32)
    prm = dict(
        shift_mu=rows(shift_mu), w0=rows(w0), a0=rows(a0), k_k=rows(k_k), k_a=rows(k_a), r_k=rows(r_k),
        wup_pad=bf(jnp.concatenate([w_up, zeros_lora], axis=1)),
        aup_pad=bf(jnp.concatenate([zeros_lora, a_up], axis=1)),
        g_up=bf(g_up), v0=rows(v0), v_down=bf(v_down), v_up=bf(v_up),
        conv_w=conv_w, conv_b=rows(conv_b),
        wax=bf(jnp.concatenate([_block_diag(lru_wa), _block_diag(lru_wx)], axis=2)),
        bax=rows(jnp.concatenate([lru_ba, lru_bx], axis=1)), lru_lambda=rows(lru_lambda),
        ln_w=rows(ln_w), ln_b=rows(ln_b),
    )
    ffn1_b = (bf(ffn1_w_up[0]), bf(ffn1_w_down[0]))
    v_first = None
    for l in range(depth):
        xf, (w_in_b, p_rwkv_b, p_lru_b, w_out_b, *ffn2_b) = _ffn(
            xf, l, FFN1_NORM_ROW, norm_g, *ffn1_b,
            to_cast=[w_in, p_rwkv, p_lru, w_out, ffn2_w_up, ffn2_w_down], cast_layer=l)
        r, lw, k, v, kk, a, bonus, gate, lru_in = _mixer_in(xf, seq, l, norm_g, w_in_b, prm, v_first, hsum)
        if l == 0:
            v_first = v
        y = _rwkv(r, lw, k, v, kk, a, batch, seq)
        xf = _mixer_out(xf, seq, y, bonus, gate, lru_in, l, norm_g, (w_in_b, p_rwkv_b, p_lru_b, w_out_b),
                        prm, hmean)
        nxt = [ffn1_w_up, ffn1_w_down] if l + 1 < depth else []
        xf, ffn1_b = _ffn(xf, l, FFN2_NORM_ROW, norm_g, *ffn2_b, to_cast=nxt, cast_layer=l + 1)
    return xf.reshape(batch, seq, d)
```

```python
import functools

import jax
import jax.numpy as jnp
from jax import lax
from jax.experimental import pallas as pl
from jax.experimental.pallas import tpu as pltpu

D_MODEL = 1024
RWKV_WIDTH = 512
HEAD_DIM = 64
LRU_WIDTH = 512
LRU_BLOCKS = 8
CONV_WIDTH = 4
LRU_C = 8.0
D_FF = 2816
RMS_EPS = 1e-6
GN_EPS = 64e-5
FFN1_NORM_ROW = 0
MIXER_NORM_ROW = 2
FFN2_NORM_ROW = 4
DECAY_LORA = 64
AAA_LORA = 64
GATE_LORA = 128
W_SHIFT = 3 * RWKV_WIDTH + DECAY_LORA + AAA_LORA + GATE_LORA
W_MIX = W_SHIFT + 2 * LRU_WIDTH
W_IN = W_MIX + 2 * D_MODEL

RWKV_CHUNKS = 8
CHUNK = 64
LANES = 128
SUBLANES = 8
BF16_ROWS = 16
FFN_TILE = 512
FFN_COLS = 256
MIX_TILE = 256
MIX_OUT_TILE = 512
MIX_OUT_ROWS = 256
PROJ_COLS = 256
PROJ_SPLIT = (2, 2, 2, 5)
VMEM_LIMIT = 56 * 1024 * 1024

BF16 = jnp.bfloat16
F32 = jnp.float32


def _dot(a, b):
    return jnp.dot(a.astype(BF16), b.astype(BF16), preferred_element_type=F32)


def _dot_nt(a, b):
    return lax.dot_general(a.astype(BF16), b.astype(BF16), (((1,), (1,)), ((), ())),
                           preferred_element_type=F32)


def _dot_tn(a, b):
    return lax.dot_general(a.astype(BF16), b.astype(BF16), (((0,), (0,)), ((), ())),
                           preferred_element_type=F32)


def _rmsnorm(x, g):
    ms = jnp.mean(x * x, axis=-1, keepdims=True)
    return x * lax.rsqrt(ms + RMS_EPS) * g


def _softplus(x, accurate):
    e = jnp.exp(-jnp.abs(x))
    return jnp.maximum(x, 0.0) + (jnp.log1p(e) if accurate else jnp.log(1.0 + e))


def _shift_rows(prev_rows, x, d):
    ext = jnp.concatenate([prev_rows, x], axis=0)
    return pltpu.roll(ext, d, 0)[SUBLANES:]


def _full(shape):
    return pl.BlockSpec(shape, lambda *_: (0,) * len(shape))


def _resident(shape):
    return pl.BlockSpec(shape, lambda *_: (0,) * len(shape), pipeline_mode=pl.Buffered(1))


def _layer(arr, l):
    shape = arr.shape[1:]
    return pl.BlockSpec((None,) + shape, lambda *_: (l,) + (0,) * len(shape),
                        pipeline_mode=pl.Buffered(1))


def _cast_job(src, l, steps):
    _, rows, cols = src.shape
    blk = next(b for b in range(BF16_ROWS, rows + 1, BF16_ROWS)
               if rows % b == 0 and b * steps >= rows)
    last = rows // blk - 1
    in_spec = pl.BlockSpec((None, blk, cols), lambda i: (l, jnp.minimum(i, last), 0))
    out_spec = pl.BlockSpec((blk, cols), lambda i: (jnp.minimum(i, last), 0))
    return in_spec, out_spec, jax.ShapeDtypeStruct((rows, cols), BF16)


def _ffn_kernel(norm_row, n_cast, x_ref, g_ref, wup_ref, wdn_ref, *refs):
    cast_src, o_ref, cast_dst = refs[:n_cast], refs[n_cast], refs[n_cast + 1:]
    x = x_ref[...]
    h = _rmsnorm(x, g_ref[norm_row:norm_row + 1, :]).astype(BF16)
    acc = jnp.zeros(x.shape, F32)
    for j in range(D_FF // FFN_COLS):
        lo = j * FFN_COLS
        gate = jnp.dot(h, wup_ref[:, lo:lo + FFN_COLS], preferred_element_type=F32)
        up = jnp.dot(h, wup_ref[:, D_FF + lo:D_FF + lo + FFN_COLS], preferred_element_type=F32)
        act = (gate * jax.nn.sigmoid(gate) * up).astype(BF16)
        acc = acc + jnp.dot(act, wdn_ref[lo:lo + FFN_COLS, :], preferred_element_type=F32)
    o_ref[...] = x + 0.5 * _rmsnorm(acc, g_ref[norm_row + 1:norm_row + 2, :])
    for src, dst in zip(cast_src, cast_dst):
        dst[...] = src[...].astype(BF16)


def _ffn(x, l, norm_row, norm_g, w_up, w_down, to_cast, cast_layer):
    n = x.shape[0]
    steps = n // FFN_TILE
    jobs = [_cast_job(src, cast_layer, steps) for src in to_cast]
    tile = pl.BlockSpec((FFN_TILE, D_MODEL), lambda i: (i, 0))
    out = pl.pallas_call(
        functools.partial(_ffn_kernel, norm_row, len(jobs)),
        grid=(steps,),
        in_specs=[tile, _layer(norm_g, l), _resident(w_up.shape), _resident(w_down.shape)]
        + [j[0] for j in jobs],
        out_specs=[tile] + [j[1] for j in jobs],
        out_shape=[jax.ShapeDtypeStruct((n, D_MODEL), F32)] + [j[2] for j in jobs],
        compiler_params=pltpu.CompilerParams(
            dimension_semantics=("arbitrary",), vmem_limit_bytes=VMEM_LIMIT),
        name="ffn",
    )(x, norm_g, w_up, w_down, *to_cast)
    return out[0], out[1:]


def _scan_rows(a, b, h0):
    n, w = a.shape
    groups = n // SUBLANES
    a = a.reshape(groups, SUBLANES, w)
    b = b.reshape(groups, SUBLANES, w)
    sub = lax.broadcasted_iota(jnp.int32, a.shape, 1)
    d = 1
    while d < SUBLANES:
        keep = sub >= d
        a_prev = jnp.where(keep, pltpu.roll(a, d, 1), 1.0)
        b_prev = jnp.where(keep, pltpu.roll(b, d, 1), 0.0)
        b = a * b_prev + b
        a = a * a_prev
        d *= 2
    out = []
    for i in range(groups):
        hi = a[i] * h0 + b[i]
        out.append(hi)
        h0 = hi[SUBLANES - 1:SUBLANES, :]
    return jnp.concatenate(out, axis=0)


def _mixer_in_kernel(has_vmix, tiles_per_seq, *refs):
    (x_ref, g_ref, win_ref, mu_ref, w0_ref, wup_ref, a0_ref, aup_ref, gup_ref,
     kk_ref, ka_ref, rk_ref) = refs[:12]
    refs = refs[12:]
    if has_vmix:
        v0_ref, vdown_ref, vup_ref, vfirst_ref = refs[:4]
        refs = refs[4:]
    (hsum_ref, r_out, lw_out, k_out, v_out, kkn_out, a_out, bonus_out, g_out, lru_out,
     proj_even, proj_odd, pcarry) = refs

    tm = x_ref.shape[0]
    step = pl.program_id(0)
    first = (step + tiles_per_seq - 1) % tiles_per_seq == 0

    @pl.when(step == 0)
    def _():
        proj_odd[...] = jnp.zeros(proj_odd.shape, F32)

    @pl.when(jnp.logical_or(first, step == 0))
    def _():
        pcarry[...] = jnp.zeros(pcarry.shape, F32)

    def body(proj_new, proj):
        h = _rmsnorm(x_ref[...], g_ref[MIXER_NORM_ROW:MIXER_NORM_ROW + 1, :]).astype(BF16)
        chunk_starts = iter(range(0, W_MIX, PROJ_COLS))

        def project(n_chunks):
            for _ in range(n_chunks):
                lo = next(chunk_starts)
                proj_new[:, lo:lo + PROJ_COLS] = jnp.dot(
                    h, win_ref[:, lo:lo + PROJ_COLS], preferred_element_type=F32)

        project(PROJ_SPLIT[0])
        p = proj[:, :W_SHIFT]
        prev = _shift_rows(pcarry[...], p, 1)
        pcarry[...] = p[tm - SUBLANES:, :]
        p = p + mu_ref[...] * (prev - p)
        rw = RWKV_WIDTH
        r, k, v = p[:, 0:rw], p[:, rw:2 * rw], p[:, 2 * rw:3 * rw]
        xwa = p[:, 3 * rw:3 * rw + DECAY_LORA + AAA_LORA]
        xg = p[:, 3 * rw + DECAY_LORA + AAA_LORA:]
        w = -_softplus(-(w0_ref[...] + _dot(jnp.tanh(xwa), wup_ref[...])), accurate=False) - 0.5
        a = jax.nn.sigmoid(a0_ref[...] + _dot(xwa, aup_ref[...]))
        g_out[...] = _dot(jax.nn.sigmoid(xg), gup_ref[...])
        if has_vmix:
            v_lora = _dot(v, vdown_ref[...])
        project(PROJ_SPLIT[1])
        lw_out[...] = -jnp.exp(w)
        if has_vmix:
            mix = jax.nn.sigmoid(v0_ref[...] + _dot(v_lora, vup_ref[...]))
            v = v + (vfirst_ref[...] - v) * mix
        kk = k * kk_ref[...]
        kk = kk * lax.rsqrt(jnp.maximum(_dot(kk * kk, hsum_ref[...]), 1e-24))
        project(PROJ_SPLIT[2])
        k = k * (1.0 + (a - 1.0) * ka_ref[...])
        r_out[...] = r
        k_out[...] = k
        v_out[...] = v
        kkn_out[...] = kk
        a_out[...] = a
        bonus_out[...] = _dot(r * k * rk_ref[...], hsum_ref[...]) * v
        project(PROJ_SPLIT[3])
        lru_out[...] = proj[:, W_SHIFT:]

    @pl.when(step % 2 == 0)
    def _():
        body(proj_even, proj_odd)

    @pl.when(step % 2 == 1)
    def _():
        body(proj_odd, proj_even)


def _mixer_in(x, seq, l, norm_g, w_in_b, prm, v_first, hsum):
    n = x.shape[0]
    tm = MIX_TILE
    tiles = n // tm
    has_vmix = v_first is not None
    x_spec = pl.BlockSpec((tm, D_MODEL), lambda i: (jnp.minimum(i, tiles - 1), 0))
    lag512 = pl.BlockSpec((tm, RWKV_WIDTH), lambda i: (jnp.maximum(i - 1, 0), 0))
    names = ["shift_mu", "w0", "wup_pad", "a0", "aup_pad", "g_up", "k_k", "k_a", "r_k"]
    w_mix_spec = pl.BlockSpec((D_MODEL, W_MIX), lambda i: (0, 0), pipeline_mode=pl.Buffered(1))
    in_specs = [x_spec, _layer(norm_g, l), w_mix_spec] + [_layer(prm[k], l) for k in names]
    args = [x, norm_g, w_in_b] + [prm[k] for k in names]
    if has_vmix:
        vnames = ["v0", "v_down", "v_up"]
        in_specs += [_layer(prm[k], l - 1) for k in vnames] + [lag512]
        args += [prm[k] for k in vnames] + [v_first]
    in_specs += [_full((RWKV_WIDTH, RWKV_WIDTH))]
    args += [hsum]
    out512 = jax.ShapeDtypeStruct((n, RWKV_WIDTH), F32)
    lag_lru = pl.BlockSpec((tm, 2 * LRU_WIDTH), lambda i: (jnp.maximum(i - 1, 0), 0))
    return pl.pallas_call(
        functools.partial(_mixer_in_kernel, has_vmix, seq // tm),
        grid=(tiles + 1,),
        in_specs=in_specs,
        out_specs=[lag512] * 8 + [lag_lru],
        out_shape=[out512] * 8 + [jax.ShapeDtypeStruct((n, 2 * LRU_WIDTH), F32)],
        scratch_shapes=[
            pltpu.VMEM((tm, W_MIX), F32),
            pltpu.VMEM((tm, W_MIX), F32),
            pltpu.VMEM((SUBLANES, W_SHIFT), F32),
        ],
        compiler_params=pltpu.CompilerParams(
            dimension_semantics=("arbitrary",), vmem_limit_bytes=VMEM_LIMIT),
        name="mixer_in",
    )(*args)


def _cumsum_rows(x):
    n, w = x.shape
    groups = n // SUBLANES
    x = x.reshape(groups, SUBLANES, w)
    sub = lax.broadcasted_iota(jnp.int32, x.shape, 1)
    d = 1
    while d < SUBLANES:
        x = x + jnp.where(sub >= d, pltpu.roll(x, d, 1), 0.0)
        d *= 2
    out = [x[0]]
    for i in range(1, groups):
        out.append(x[i] + out[-1][SUBLANES - 1:SUBLANES, :])
    return jnp.concatenate(out, axis=0)


def _rwkv_kernel(r_ref, lw_ref, k_ref, v_ref, kk_ref, a_ref, y_ref, s_ref):
    c = CHUNK
    nb = r_ref.shape[0]
    pairs = RWKV_WIDTH // LANES
    chains = [(bi, p) for bi in range(nb) for p in range(pairs)]

    @pl.when(pl.program_id(0) == 0)
    def _():
        s_ref[...] = jnp.zeros(s_ref.shape, F32)

    lane = lax.broadcasted_iota(jnp.int32, (c, LANES), 1)
    t_idx = lax.broadcasted_iota(jnp.int32, (c, LANES), 0)
    left = lane < HEAD_DIM
    s_idx = jnp.where(left, lane, lane - HEAD_DIM)
    strict = t_idx > s_idx
    incl = t_idx >= s_idx
    eye_w = (t_idx == s_idx).astype(F32)
    row2 = lax.broadcasted_iota(jnp.int32, (LANES, LANES), 0)
    lane2 = lax.broadcasted_iota(jnp.int32, (LANES, LANES), 1)
    same_head = (row2 < HEAD_DIM) == (lane2 < HEAD_DIM)
    eye2 = row2 == lane2

    def bd(w):
        return jnp.concatenate([jnp.where(left, w, 0.0), jnp.where(left, 0.0, w)], axis=0)

    def prep(g):
        ops = {}
        rows = slice(g * c, (g + 1) * c)
        for bi in range(nb):
            lw = lw_ref[bi, rows, :]
            cum = _cumsum_rows(lw)
            last = cum[c - 1:c, :]
            kk = kk_ref[bi, rows, :]
            k = k_ref[bi, rows, :]
            b = kk * a_ref[bi, rows, :]
            e_neg = jnp.exp(-cum)
            e_tail = jnp.exp(last - cum)
            a_t = -kk * jnp.exp(cum - lw)
            r_t = r_ref[bi, rows, :] * jnp.exp(cum)
            b_t = b * e_neg
            k_t = k * e_neg
            b_h = b * e_tail
            k_h = k * e_tail
            w_c = jnp.exp(last)
            v = v_ref[bi, rows, :]
            for p in range(pairs):
                sl = slice(p * LANES, (p + 1) * LANES)
                w_col = jnp.sum(jnp.where(eye2, w_c[:, sl], 0.0), axis=1, keepdims=True)
                ops[bi, p] = dict(a=a_t[:, sl], r=r_t[:, sl], bt=b_t[:, sl], kt=k_t[:, sl],
                                  bh=b_h[:, sl], kh=k_h[:, sl], v=v[:, sl], w_col=w_col)
        for ch in chains:
            o = ops[ch]
            m = _dot_nt(jnp.concatenate([o["a"], o["r"]], axis=0),
                        jnp.concatenate([bd(o["bt"]), bd(o["kt"])], axis=0))
            o["a_ab"] = jnp.where(strict, m[:c, :LANES], 0.0)
            o["a_ak"] = jnp.where(strict, m[:c, LANES:], 0.0)
            o["a_rb"] = jnp.where(incl, m[c:, :LANES], 0.0)
            o["a_rk"] = jnp.where(incl, m[c:, LANES:], 0.0)
        return ops

    def inverse_stages(ops):
        def first():
            for ch in chains:
                o = ops[ch]
                o["t"] = eye_w + o["a_ab"]
                o["p"] = _dot(o["a_ab"], bd(o["a_ab"]))

        def double(is_last):
            def run():
                for ch in chains:
                    o = ops[ch]
                    p_bd = bd(o["p"])
                    if is_last:
                        o["t"] = o["t"] + _dot(o["t"], p_bd)
                    else:
                        prod = _dot(jnp.concatenate([o["t"], o["p"]], axis=0), p_bd)
                        o["t"] = o["t"] + prod[:c]
                        o["p"] = prod[c:]
            return run

        steps = CHUNK.bit_length() - 2
        return [first] + [double(i == steps - 1) for i in range(steps)]

    def state_stages(ops, state, g):
        rows = slice(g * c, (g + 1) * c)

        def stage_x():
            for ch in chains:
                o = ops[ch]
                o["v_bd"] = bd(o["v"])
                o["x"] = _dot(jnp.concatenate([o["a"], o["a_ak"]], axis=1),
                              jnp.concatenate([state[ch], o["v_bd"]], axis=0))

        def stage_u():
            for ch in chains:
                o = ops[ch]
                o["u"] = _dot(o["t"], bd(o["x"]))

        def stage_s():
            for ch in chains:
                o = ops[ch]
                ds = _dot_tn(jnp.concatenate([o["bh"], o["kh"]], axis=0),
                             jnp.concatenate([o["u"], o["v"]], axis=0))
                o["s_new"] = o["w_col"] * state[ch] + jnp.where(same_head, ds, 0.0)

        def stage_y():
            for ch in chains:
                o = ops[ch]
                bi, p = ch
                y_ref[bi, rows, p * LANES:(p + 1) * LANES] = _dot(
                    jnp.concatenate([o["r"], o["a_rb"], o["a_rk"]], axis=1),
                    jnp.concatenate([state[ch], bd(o["u"]), o["v_bd"]], axis=0))
                state[ch] = o["s_new"]

        return [stage_x, stage_u, stage_s, stage_y]

    state = {ch: s_ref[ch[0], ch[1]] for ch in chains}
    ops_next = prep(0)
    pending = []
    for g in range(RWKV_CHUNKS):
        ops_cur = ops_next
        inv = inverse_stages(ops_cur)
        half = len(inv) // 2
        for i, stage in enumerate(inv):
            stage()
            if i < len(pending):
                pending[i]()
            if i == half and g + 1 < RWKV_CHUNKS:
                ops_next = prep(g + 1)
        for stage in pending[len(inv):]:
            stage()
        pending = state_stages(ops_cur, state, g)
    for stage in pending:
        stage()
    for ch, s_val in state.items():
        s_ref[ch[0], ch[1]] = s_val


def _rwkv(r, lw, k, v, kk, a, batch, seq):
    rows = RWKV_CHUNKS * CHUNK
    spec = pl.BlockSpec((batch, rows, RWKV_WIDTH), lambda i: (0, i, 0))
    shape3 = (batch, seq, RWKV_WIDTH)
    y = pl.pallas_call(
        _rwkv_kernel,
        grid=(seq // rows,),
        in_specs=[spec] * 6,
        out_specs=spec,
        out_shape=jax.ShapeDtypeStruct(shape3, F32),
        scratch_shapes=[pltpu.VMEM((batch, RWKV_WIDTH // LANES, LANES, LANES), F32)],
        compiler_params=pltpu.CompilerParams(
            dimension_semantics=("arbitrary",), vmem_limit_bytes=VMEM_LIMIT),
        name="rwkv",
    )(*(t.reshape(shape3) for t in (r, lw, k, v, kk, a)))
    return y.reshape(batch * seq, RWKV_WIDTH)


def _mixer_out_kernel(tiles_per_seq, x_ref, y_ref, bonus_ref, g_ref, lru_ref, gnorm_ref, win_ref, prw_ref,
                      plr_ref, wout_ref, lnw_ref, lnb_ref, convw_ref, convb_ref, wax_ref, bax_ref, lam_ref,
                      hmean_ref, o_ref, xcarry, hcarry):
    tm = x_ref.shape[0]
    first = pl.program_id(0) % tiles_per_seq == 0

    @pl.when(first)
    def _():
        xcarry[...] = jnp.zeros(xcarry.shape, F32)
        hcarry[...] = jnp.zeros(hcarry.shape, F32)

    blocks = [slice(i * MIX_OUT_ROWS, (i + 1) * MIX_OUT_ROWS) for i in range(tm // MIX_OUT_ROWS)]
    st = [dict() for _ in blocks]
    lw_ = LRU_WIDTH
    for s, rs in zip(st, blocks):
        s["x"] = x_ref[rs, :]
        s["hx"] = _rmsnorm(s["x"], gnorm_ref[MIXER_NORM_ROW:MIXER_NORM_ROW + 1, :]).astype(BF16)
        s["y"] = y_ref[rs, :]
        s["mu"] = _dot(s["y"], hmean_ref[...])

    gate_jobs = [(s, name, W_MIX + j * D_MODEL) for s in st for j, name in enumerate(("gate_r", "gate_l"))]

    def gate_matmuls(count):
        for s, name, lo in gate_jobs[:count]:
            s[name] = jax.nn.sigmoid(jnp.dot(s["hx"], win_ref[:, lo:lo + D_MODEL], preferred_element_type=F32))
        del gate_jobs[:count]

    gate_matmuls(1)

    xprev = xcarry[...]
    for s, rs in zip(st, blocks):
        lx = lru_ref[rs, :lw_]
        xb = convb_ref[...] + convw_ref[CONV_WIDTH - 1:CONV_WIDTH, :] * lx
        for d in range(1, CONV_WIDTH):
            xb = xb + convw_ref[CONV_WIDTH - 1 - d:CONV_WIDTH - d, :] * _shift_rows(xprev, lx, d)
        xprev = lx[MIX_OUT_ROWS - SUBLANES:, :]
        s["xb"] = xb
        s["lgates"] = _dot(xb, wax_ref[...]) + bax_ref[...]
    xcarry[...] = xprev
    gate_matmuls(1)

    for s, rs in zip(st, blocks):
        s["yc"] = s["y"] - s["mu"]
        s["var"] = _dot(s["yc"] * s["yc"], hmean_ref[...])
    gate_matmuls(1)
    for s, rs in zip(st, blocks):
        yn = s["yc"] * lax.rsqrt(s["var"] + GN_EPS) * lnw_ref[...] + lnb_ref[...] + bonus_ref[rs, :]
        s["rwkv_proj"] = _dot(yn * g_ref[rs, :], prw_ref[...])
    gate_matmuls(len(gate_jobs))

    h_prev = hcarry[SUBLANES - 1:SUBLANES, :]
    for i, (s, rs) in enumerate(zip(st, blocks)):
        gate_a = jax.nn.sigmoid(s["lgates"][:, :lw_])
        gate_x = jax.nn.sigmoid(s["lgates"][:, lw_:])
        log_a = -LRU_C * gate_a * _softplus(-lam_ref[...], accurate=True)
        a_l = jnp.exp(log_a)
        mult = jnp.sqrt(-jnp.tanh(log_a) * (a_l * a_l + 1.0))
        if i == 0:
            row = lax.broadcasted_iota(jnp.int32, log_a.shape, 0)
            seq_start = row + jnp.where(first, 0, 1) == 0
            mult = jnp.where(seq_start, 1.0, mult)
        hseq = _scan_rows(a_l, s["xb"] * gate_x * mult, h_prev)
        h_prev = hseq[MIX_OUT_ROWS - 1:, :]
        if i == len(blocks) - 1:
            hcarry[...] = hseq[MIX_OUT_ROWS - SUBLANES:, :]
        s["lru_proj"] = _dot(hseq * jax.nn.gelu(lru_ref[rs, lw_:]), plr_ref[...])

    for s, rs in zip(st, blocks):
        merged = s["gate_r"] * s["rwkv_proj"] + s["gate_l"] * s["lru_proj"]
        out = _dot(merged, wout_ref[...])
        o_ref[rs, :] = s["x"] + _rmsnorm(out, gnorm_ref[MIXER_NORM_ROW + 1:MIXER_NORM_ROW + 2, :])


def _mixer_out(x, seq, y, bonus, g, lru_in, l, norm_g, weights, prm, hmean):
    n = x.shape[0]
    tm = MIX_OUT_TILE
    row512 = pl.BlockSpec((tm, RWKV_WIDTH), lambda i: (i, 0))
    row1024 = pl.BlockSpec((tm, D_MODEL), lambda i: (i, 0))
    names = ["ln_w", "ln_b", "conv_w", "conv_b", "wax", "bax", "lru_lambda"]
    return pl.pallas_call(
        functools.partial(_mixer_out_kernel, seq // tm),
        grid=(n // tm,),
        in_specs=([row1024, row512, row512, row512, row1024, _layer(norm_g, l)]
                  + [_resident(w.shape) for w in weights]
                  + [_layer(prm[k], l) for k in names] + [_full((RWKV_WIDTH, RWKV_WIDTH))]),
        out_specs=row1024,
        out_shape=jax.ShapeDtypeStruct((n, D_MODEL), F32),
        scratch_shapes=[
            pltpu.VMEM((SUBLANES, LRU_WIDTH), F32),
            pltpu.VMEM((SUBLANES, LRU_WIDTH), F32),
        ],
        compiler_params=pltpu.CompilerParams(
            dimension_semantics=("arbitrary",), vmem_limit_bytes=VMEM_LIMIT),
        name="mixer_out",
    )(x, y, bonus, g, lru_in, norm_g, *weights, *[prm[k] for k in names], hmean)


def _block_diag(w):
    nl, nb, d, e = w.shape
    eye = jnp.eye(nb, dtype=w.dtype)
    return jnp.einsum("lnde,nm->lndme", w, eye).reshape(nl, nb * d, nb * e)


def kernel(x, norm_g, ffn1_w_up, ffn1_w_down, ffn2_w_up, ffn2_w_down, w_in, shift_mu, w0, w_up, a0, a_up, g_up, k_k, k_a, r_k, ln_w, ln_b, v0, v_down, v_up, conv_w, conv_b, lru_wa, lru_ba, lru_wx, lru_bx, lru_lambda, p_rwkv, p_lru, w_out):
    batch, seq, d = x.shape
    depth = norm_g.shape[0]
    assert d == D_MODEL and seq % MIX_TILE == 0 and (batch * seq) % FFN_TILE == 0
    assert seq % (RWKV_CHUNKS * CHUNK) == 0 and seq % MIX_OUT_TILE == 0
    xf = x.reshape(batch * seq, d)
    bf = lambda t: t.astype(BF16)
    rows = lambda t: t.reshape(t.shape[0], 1, -1)
    head = jnp.arange(RWKV_WIDTH) // HEAD_DIM
    same = (head[:, None] == head[None, :])
    hsum = same.astype(BF16)
    hmean = (same.astype(F32) / HEAD_DIM).astype(BF16)
    zeros_lora = jnp.zeros((depth, DECAY_LORA, RWKV_WIDTH), F32)
    prm = dict(
        shift_mu=rows(shift_mu), w0=rows(w0), a0=rows(a0), k_k=rows(k_k), k_a=rows(k_a), r_k=rows(r_k),
        wup_pad=bf(jnp.concatenate([w_up, zeros_lora], axis=1)),
        aup_pad=bf(jnp.concatenate([zeros_lora, a_up], axis=1)),
        g_up=bf(g_up), v0=rows(v0), v_down=bf(v_down), v_up=bf(v_up),
        conv_w=conv_w, conv_b=rows(conv_b),
        wax=bf(jnp.concatenate([_block_diag(lru_wa), _block_diag(lru_wx)], axis=2)),
        bax=rows(jnp.concatenate([lru_ba, lru_bx], axis=1)), lru_lambda=rows(lru_lambda),
        ln_w=rows(ln_w), ln_b=rows(ln_b),
    )
    ffn1_b = (bf(ffn1_w_up[0]), bf(ffn1_w_down[0]))
    v_first = None
    for l in range(depth):
        xf, (w_in_b, p_rwkv_b, p_lru_b, w_out_b, *ffn2_b) = _ffn(
            xf, l, FFN1_NORM_ROW, norm_g, *ffn1_b,
            to_cast=[w_in, p_rwkv, p_lru, w_out, ffn2_w_up, ffn2_w_down], cast_layer=l)
        r, lw, k, v, kk, a, bonus, gate, lru_in = _mixer_in(xf, seq, l, norm_g, w_in_b, prm, v_first, hsum)
        if l == 0:
            v_first = v
        y = _rwkv(r, lw, k, v, kk, a, batch, seq)
        xf = _mixer_out(xf, seq, y, bonus, gate, lru_in, l, norm_g, (w_in_b, p_rwkv_b, p_lru_b, w_out_b),
                        prm, hmean)
        nxt = [ffn1_w_up, ffn1_w_down] if l + 1 < depth else []
        xf, ffn1_b = _ffn(xf, l, FFN2_NORM_ROW, norm_g, *ffn2_b, to_cast=nxt, cast_layer=l + 1)
    return xf.reshape(batch, seq, d)
```

```python
import functools

import jax
import jax.numpy as jnp
from jax import lax
from jax.experimental import pallas as pl
from jax.experimental.pallas import tpu as pltpu

D_MODEL = 1024
RWKV_WIDTH = 512
HEAD_DIM = 64
LRU_WIDTH = 512
LRU_BLOCKS = 8
CONV_WIDTH = 4
LRU_C = 8.0
D_FF = 2816
RMS_EPS = 1e-6
GN_EPS = 64e-5
FFN1_NORM_ROW = 0
MIXER_NORM_ROW = 2
FFN2_NORM_ROW = 4
DECAY_LORA = 64
AAA_LORA = 64
GATE_LORA = 128
W_SHIFT = 3 * RWKV_WIDTH + DECAY_LORA + AAA_LORA + GATE_LORA
W_MIX = W_SHIFT + 2 * LRU_WIDTH
W_IN = W_MIX + 2 * D_MODEL

RWKV_CHUNKS = 8
CHUNK = 64
LANES = 128
SUBLANES = 8
BF16_ROWS = 16
FFN_TILE = 512
FFN_COLS = 256
MIX_TILE = 512
MIX_OUT_TILE = 512
MIX_OUT_ROWS = 256
PROJ_COLS = 256
PROJ_SPLIT = (2, 2, 2, 5)
VMEM_LIMIT = 56 * 1024 * 1024

BF16 = jnp.bfloat16
F32 = jnp.float32


def _dot(a, b):
    return jnp.dot(a.astype(BF16), b.astype(BF16), preferred_element_type=F32)


def _dot_nt(a, b):
    return lax.dot_general(a.astype(BF16), b.astype(BF16), (((1,), (1,)), ((), ())),
                           preferred_element_type=F32)


def _dot_tn(a, b):
    return lax.dot_general(a.astype(BF16), b.astype(BF16), (((0,), (0,)), ((), ())),
                           preferred_element_type=F32)


def _rmsnorm(x, g):
    ms = jnp.mean(x * x, axis=-1, keepdims=True)
    return x * lax.rsqrt(ms + RMS_EPS) * g


def _softplus(x, accurate):
    e = jnp.exp(-jnp.abs(x))
    return jnp.maximum(x, 0.0) + (jnp.log1p(e) if accurate else jnp.log(1.0 + e))


def _shift_rows(prev_rows, x, d):
    ext = jnp.concatenate([prev_rows, x], axis=0)
    return pltpu.roll(ext, d, 0)[SUBLANES:]


def _full(shape):
    return pl.BlockSpec(shape, lambda *_: (0,) * len(shape))


def _resident(shape):
    return pl.BlockSpec(shape, lambda *_: (0,) * len(shape), pipeline_mode=pl.Buffered(1))


def _layer(arr, l):
    shape = arr.shape[1:]
    return pl.BlockSpec((None,) + shape, lambda *_: (l,) + (0,) * len(shape),
                        pipeline_mode=pl.Buffered(1))


def _rows(arr):
    return pl.BlockSpec(arr.shape, lambda *_: (0, 0), pipeline_mode=pl.Buffered(1))


def _cast_job(src, l, steps):
    _, rows, cols = src.shape
    blk = next(b for b in range(BF16_ROWS, rows + 1, BF16_ROWS)
               if rows % b == 0 and b * steps >= rows)
    last = rows // blk - 1
    in_spec = pl.BlockSpec((None, blk, cols), lambda i: (l, jnp.minimum(i, last), 0))
    out_spec = pl.BlockSpec((blk, cols), lambda i: (jnp.minimum(i, last), 0))
    return in_spec, out_spec, jax.ShapeDtypeStruct((rows, cols), BF16)


def _ffn_kernel(norm_row, n_cast, x_ref, g_ref, wup_ref, wdn_ref, *refs):
    cast_src, o_ref, cast_dst = refs[:n_cast], refs[n_cast], refs[n_cast + 1:]
    x = x_ref[...]
    h = _rmsnorm(x, g_ref[norm_row:norm_row + 1, :]).astype(BF16)
    acc = jnp.zeros(x.shape, F32)
    for j in range(D_FF // FFN_COLS):
        lo = j * FFN_COLS
        gate = jnp.dot(h, wup_ref[:, lo:lo + FFN_COLS], preferred_element_type=F32)
        up = jnp.dot(h, wup_ref[:, D_FF + lo:D_FF + lo + FFN_COLS], preferred_element_type=F32)
        act = (gate * jax.nn.sigmoid(gate) * up).astype(BF16)
        acc = acc + jnp.dot(act, wdn_ref[lo:lo + FFN_COLS, :], preferred_element_type=F32)
    o_ref[...] = x + 0.5 * _rmsnorm(acc, g_ref[norm_row + 1:norm_row + 2, :])
    for src, dst in zip(cast_src, cast_dst):
        dst[...] = src[...].astype(BF16)


def _ffn(x, l, norm_row, norm_g, w_up, w_down, to_cast, cast_layer):
    n = x.shape[0]
    steps = n // FFN_TILE
    jobs = [_cast_job(src, cast_layer, steps) for src in to_cast]
    tile = pl.BlockSpec((FFN_TILE, D_MODEL), lambda i: (i, 0))
    out = pl.pallas_call(
        functools.partial(_ffn_kernel, norm_row, len(jobs)),
        grid=(steps,),
        in_specs=[tile, _layer(norm_g, l), _resident(w_up.shape), _resident(w_down.shape)]
        + [j[0] for j in jobs],
        out_specs=[tile] + [j[1] for j in jobs],
        out_shape=[jax.ShapeDtypeStruct((n, D_MODEL), F32)] + [j[2] for j in jobs],
        compiler_params=pltpu.CompilerParams(
            dimension_semantics=("arbitrary",), vmem_limit_bytes=VMEM_LIMIT),
        name="ffn",
    )(x, norm_g, w_up, w_down, *to_cast)
    return out[0], out[1:]


def _scan_rows(a, b, h0):
    n, w = a.shape
    groups = n // SUBLANES
    a = a.reshape(groups, SUBLANES, w)
    b = b.reshape(groups, SUBLANES, w)
    sub = lax.broadcasted_iota(jnp.int32, a.shape, 1)
    d = 1
    while d < SUBLANES:
        keep = sub >= d
        a_prev = jnp.where(keep, pltpu.roll(a, d, 1), 1.0)
        b_prev = jnp.where(keep, pltpu.roll(b, d, 1), 0.0)
        b = a * b_prev + b
        a = a * a_prev
        d *= 2
    out = []
    for i in range(groups):
        hi = a[i] * h0 + b[i]
        out.append(hi)
        h0 = hi[SUBLANES - 1:SUBLANES, :]
    return jnp.concatenate(out, axis=0)


def _mixer_in_kernel(layer, has_vmix, tiles_per_seq, *refs):
    (x_ref, g_ref, win_ref, mu_ref, w0_ref, wup_ref, a0_ref, aup_ref, gup_ref,
     kk_ref, ka_ref, rk_ref) = refs[:12]
    refs = refs[12:]
    if has_vmix:
        v0_ref, vdown_ref, vup_ref, vfirst_ref = refs[:4]
        refs = refs[4:]
    (hsum_ref, r_out, lw_out, k_out, v_out, kkn_out, a_out, bonus_out, g_out, lru_out,
     proj_even, proj_odd, pcarry) = refs

    tm = x_ref.shape[0]
    step = pl.program_id(0)
    row = lambda ref, l=layer: ref[l:l + 1, :]
    first = (step + tiles_per_seq - 1) % tiles_per_seq == 0

    @pl.when(step == 0)
    def _():
        proj_odd[...] = jnp.zeros(proj_odd.shape, F32)

    @pl.when(jnp.logical_or(first, step == 0))
    def _():
        pcarry[...] = jnp.zeros(pcarry.shape, F32)

    def body(proj_new, proj):
        h = _rmsnorm(x_ref[...], g_ref[MIXER_NORM_ROW:MIXER_NORM_ROW + 1, :]).astype(BF16)
        chunk_starts = iter(range(0, W_MIX, PROJ_COLS))

        def project(n_chunks):
            for _ in range(n_chunks):
                lo = next(chunk_starts)
                proj_new[:, lo:lo + PROJ_COLS] = jnp.dot(
                    h, win_ref[:, lo:lo + PROJ_COLS], preferred_element_type=F32)

        project(PROJ_SPLIT[0])
        p = proj[:, :W_SHIFT]
        prev = _shift_rows(pcarry[...], p, 1)
        pcarry[...] = p[tm - SUBLANES:, :]
        p = p + row(mu_ref) * (prev - p)
        rw = RWKV_WIDTH
        r, k, v = p[:, 0:rw], p[:, rw:2 * rw], p[:, 2 * rw:3 * rw]
        xwa = p[:, 3 * rw:3 * rw + DECAY_LORA + AAA_LORA]
        xg = p[:, 3 * rw + DECAY_LORA + AAA_LORA:]
        w = -_softplus(-(row(w0_ref) + _dot(jnp.tanh(xwa), wup_ref[...])), accurate=False) - 0.5
        a = jax.nn.sigmoid(row(a0_ref) + _dot(xwa, aup_ref[...]))
        g_out[...] = _dot(jax.nn.sigmoid(xg), gup_ref[...])
        if has_vmix:
            v_lora = _dot(v, vdown_ref[...])
        project(PROJ_SPLIT[1])
        lw_out[...] = -jnp.exp(w)
        if has_vmix:
            mix = jax.nn.sigmoid(row(v0_ref, layer - 1) + _dot(v_lora, vup_ref[...]))
            v = v + (vfirst_ref[...] - v) * mix
        kk = k * row(kk_ref)
        kk = kk * lax.rsqrt(jnp.maximum(_dot(kk * kk, hsum_ref[...]), 1e-24))
        project(PROJ_SPLIT[2])
        k = k * (1.0 + (a - 1.0) * row(ka_ref))
        r_out[...] = r
        k_out[...] = k
        v_out[...] = v
        kkn_out[...] = kk
        a_out[...] = a
        bonus_out[...] = _dot(r * k * row(rk_ref), hsum_ref[...]) * v
        project(PROJ_SPLIT[3])
        lru_out[...] = proj[:, W_SHIFT:]

    @pl.when(step % 2 == 0)
    def _():
        body(proj_even, proj_odd)

    @pl.when(step % 2 == 1)
    def _():
        body(proj_odd, proj_even)


def _mixer_in(x, seq, l, norm_g, w_in_b, prm, v_first, hsum):
    n = x.shape[0]
    tm = MIX_TILE
    tiles = n // tm
    has_vmix = v_first is not None
    x_spec = pl.BlockSpec((tm, D_MODEL), lambda i: (jnp.minimum(i, tiles - 1), 0))
    lag512 = pl.BlockSpec((tm, RWKV_WIDTH), lambda i: (jnp.maximum(i - 1, 0), 0))
    names = ["shift_mu", "w0", "wup_pad", "a0", "aup_pad", "g_up", "k_k", "k_a", "r_k"]
    spec = lambda k, layer: _rows(prm[k]) if prm[k].ndim == 2 else _layer(prm[k], layer)
    w_mix_spec = pl.BlockSpec((D_MODEL, W_MIX), lambda i: (0, 0), pipeline_mode=pl.Buffered(1))
    in_specs = [x_spec, _layer(norm_g, l), w_mix_spec] + [spec(k, l) for k in names]
    args = [x, norm_g, w_in_b] + [prm[k] for k in names]
    if has_vmix:
        vnames = ["v0", "v_down", "v_up"]
        in_specs += [spec(k, l - 1) for k in vnames] + [lag512]
        args += [prm[k] for k in vnames] + [v_first]
    in_specs += [_full((RWKV_WIDTH, RWKV_WIDTH))]
    args += [hsum]
    out512 = jax.ShapeDtypeStruct((n, RWKV_WIDTH), F32)
    lag_lru = pl.BlockSpec((tm, 2 * LRU_WIDTH), lambda i: (jnp.maximum(i - 1, 0), 0))
    return pl.pallas_call(
        functools.partial(_mixer_in_kernel, l, has_vmix, seq // tm),
        grid=(tiles + 1,),
        in_specs=in_specs,
        out_specs=[lag512] * 8 + [lag_lru],
        out_shape=[out512] * 8 + [jax.ShapeDtypeStruct((n, 2 * LRU_WIDTH), F32)],
        scratch_shapes=[
            pltpu.VMEM((tm, W_MIX), F32),
            pltpu.VMEM((tm, W_MIX), F32),
            pltpu.VMEM((SUBLANES, W_SHIFT), F32),
        ],
        compiler_params=pltpu.CompilerParams(
            dimension_semantics=("arbitrary",), vmem_limit_bytes=VMEM_LIMIT),
        name="mixer_in",
    )(*args)


def _cumsum_rows(x):
    n, w = x.shape
    groups = n // SUBLANES
    x = x.reshape(groups, SUBLANES, w)
    sub = lax.broadcasted_iota(jnp.int32, x.shape, 1)
    d = 1
    while d < SUBLANES:
        x = x + jnp.where(sub >= d, pltpu.roll(x, d, 1), 0.0)
        d *= 2
    out = [x[0]]
    for i in range(1, groups):
        out.append(x[i] + out[-1][SUBLANES - 1:SUBLANES, :])
    return jnp.concatenate(out, axis=0)


def _rwkv_kernel(r_ref, lw_ref, k_ref, v_ref, kk_ref, a_ref, y_ref, s_ref):
    c = CHUNK
    nb = r_ref.shape[0]
    pairs = RWKV_WIDTH // LANES
    chains = [(bi, p) for bi in range(nb) for p in range(pairs)]

    @pl.when(pl.program_id(0) == 0)
    def _():
        s_ref[...] = jnp.zeros(s_ref.shape, F32)

    lane = lax.broadcasted_iota(jnp.int32, (c, LANES), 1)
    t_idx = lax.broadcasted_iota(jnp.int32, (c, LANES), 0)
    left = lane < HEAD_DIM
    s_idx = jnp.where(left, lane, lane - HEAD_DIM)
    strict = t_idx > s_idx
    incl = t_idx >= s_idx
    eye_w = (t_idx == s_idx).astype(F32)
    row2 = lax.broadcasted_iota(jnp.int32, (LANES, LANES), 0)
    lane2 = lax.broadcasted_iota(jnp.int32, (LANES, LANES), 1)
    same_head = (row2 < HEAD_DIM) == (lane2 < HEAD_DIM)
    eye2 = row2 == lane2

    def bd(w):
        return jnp.concatenate([jnp.where(left, w, 0.0), jnp.where(left, 0.0, w)], axis=0)

    def prep(g):
        ops = {}
        rows = slice(g * c, (g + 1) * c)
        for bi in range(nb):
            lw = lw_ref[bi, rows, :]
            cum = _cumsum_rows(lw)
            last = cum[c - 1:c, :]
            kk = kk_ref[bi, rows, :]
            k = k_ref[bi, rows, :]
            b = kk * a_ref[bi, rows, :]
            e_neg = jnp.exp(-cum)
            e_tail = jnp.exp(last - cum)
            a_t = -kk * jnp.exp(cum - lw)
            r_t = r_ref[bi, rows, :] * jnp.exp(cum)
            b_t = b * e_neg
            k_t = k * e_neg
            b_h = b * e_tail
            k_h = k * e_tail
            w_c = jnp.exp(last)
            v = v_ref[bi, rows, :]
            for p in range(pairs):
                sl = slice(p * LANES, (p + 1) * LANES)
                w_col = jnp.sum(jnp.where(eye2, w_c[:, sl], 0.0), axis=1, keepdims=True)
                ops[bi, p] = dict(a=a_t[:, sl], r=r_t[:, sl], bt=b_t[:, sl], kt=k_t[:, sl],
                                  bh=b_h[:, sl], kh=k_h[:, sl], v=v[:, sl], w_col=w_col)
        for ch in chains:
            o = ops[ch]
            m = _dot_nt(jnp.concatenate([o["a"], o["r"]], axis=0),
                        jnp.concatenate([bd(o["bt"]), bd(o["kt"])], axis=0))
            o["a_ab"] = jnp.where(strict, m[:c, :LANES], 0.0)
            o["a_ak"] = jnp.where(strict, m[:c, LANES:], 0.0)
            o["a_rb"] = jnp.where(incl, m[c:, :LANES], 0.0)
            o["a_rk"] = jnp.where(incl, m[c:, LANES:], 0.0)
        return ops

    def inverse_stages(ops):
        def first():
            for ch in chains:
                o = ops[ch]
                o["t"] = eye_w + o["a_ab"]
                o["p"] = _dot(o["a_ab"], bd(o["a_ab"]))

        def double(is_last):
            def run():
                for ch in chains:
                    o = ops[ch]
                    p_bd = bd(o["p"])
                    if is_last:
                        o["t"] = o["t"] + _dot(o["t"], p_bd)
                    else:
                        prod = _dot(jnp.concatenate([o["t"], o["p"]], axis=0), p_bd)
                        o["t"] = o["t"] + prod[:c]
                        o["p"] = prod[c:]
            return run

        steps = CHUNK.bit_length() - 2
        return [first] + [double(i == steps - 1) for i in range(steps)]

    def state_stages(ops, state, g):
        rows = slice(g * c, (g + 1) * c)

        def stage_x():
            for ch in chains:
                o = ops[ch]
                o["v_bd"] = bd(o["v"])
                o["x"] = _dot(jnp.concatenate([o["a"], o["a_ak"]], axis=1),
                              jnp.concatenate([state[ch], o["v_bd"]], axis=0))

        def stage_u():
            for ch in chains:
                o = ops[ch]
                o["u"] = _dot(o["t"], bd(o["x"]))

        def stage_s():
            for ch in chains:
                o = ops[ch]
                ds = _dot_tn(jnp.concatenate([o["bh"], o["kh"]], axis=0),
                             jnp.concatenate([o["u"], o["v"]], axis=0))
                o["s_new"] = o["w_col"] * state[ch] + jnp.where(same_head, ds, 0.0)

        def stage_y():
            for ch in chains:
                o = ops[ch]
                bi, p = ch
                y_ref[bi, rows, p * LANES:(p + 1) * LANES] = _dot(
                    jnp.concatenate([o["r"], o["a_rb"], o["a_rk"]], axis=1),
                    jnp.concatenate([state[ch], bd(o["u"]), o["v_bd"]], axis=0))
                state[ch] = o["s_new"]

        return [stage_x, stage_u, stage_s, stage_y]

    state = {ch: s_ref[ch[0], ch[1]] for ch in chains}
    ops_next = prep(0)
    pending = []
    for g in range(RWKV_CHUNKS):
        ops_cur = ops_next
        inv = inverse_stages(ops_cur)
        half = len(inv) // 2
        for i, stage in enumerate(inv):
            stage()
            if i < len(pending):
                pending[i]()
            if i == half and g + 1 < RWKV_CHUNKS:
                ops_next = prep(g + 1)
        for stage in pending[len(inv):]:
            stage()
        pending = state_stages(ops_cur, state, g)
    for stage in pending:
        stage()
    for ch, s_val in state.items():
        s_ref[ch[0], ch[1]] = s_val


def _rwkv(r, lw, k, v, kk, a, batch, seq):
    rows = RWKV_CHUNKS * CHUNK
    spec = pl.BlockSpec((batch, rows, RWKV_WIDTH), lambda i: (0, i, 0))
    shape3 = (batch, seq, RWKV_WIDTH)
    y = pl.pallas_call(
        _rwkv_kernel,
        grid=(seq // rows,),
        in_specs=[spec] * 6,
        out_specs=spec,
        out_shape=jax.ShapeDtypeStruct(shape3, F32),
        scratch_shapes=[pltpu.VMEM((batch, RWKV_WIDTH // LANES, LANES, LANES), F32)],
        compiler_params=pltpu.CompilerParams(
            dimension_semantics=("arbitrary",), vmem_limit_bytes=VMEM_LIMIT),
        name="rwkv",
    )(*(t.reshape(shape3) for t in (r, lw, k, v, kk, a)))
    return y.reshape(batch * seq, RWKV_WIDTH)


def _mixer_out_kernel(layer, tiles_per_seq, x_ref, y_ref, bonus_ref, g_ref, lru_ref, gnorm_ref, win_ref, prw_ref,
                      plr_ref, wout_ref, lnw_ref, lnb_ref, convw_ref, convb_ref, wax_ref, bax_ref, lam_ref,
                      hmean_ref, o_ref, xcarry, hcarry):
    tm = x_ref.shape[0]
    first = pl.program_id(0) % tiles_per_seq == 0
    row = lambda ref: ref[layer:layer + 1, :]

    @pl.when(first)
    def _():
        xcarry[...] = jnp.zeros(xcarry.shape, F32)
        hcarry[...] = jnp.zeros(hcarry.shape, F32)

    blocks = [slice(i * MIX_OUT_ROWS, (i + 1) * MIX_OUT_ROWS) for i in range(tm // MIX_OUT_ROWS)]
    st = [dict() for _ in blocks]
    lw_ = LRU_WIDTH
    for s, rs in zip(st, blocks):
        s["x"] = x_ref[rs, :]
        s["hx"] = _rmsnorm(s["x"], gnorm_ref[MIXER_NORM_ROW:MIXER_NORM_ROW + 1, :]).astype(BF16)
        s["y"] = y_ref[rs, :]
        s["mu"] = _dot(s["y"], hmean_ref[...])

    gate_jobs = [(s, name, W_MIX + j * D_MODEL) for s in st for j, name in enumerate(("gate_r", "gate_l"))]

    def gate_matmuls(count):
        for s, name, lo in gate_jobs[:count]:
            s[name] = jax.nn.sigmoid(jnp.dot(s["hx"], win_ref[:, lo:lo + D_MODEL], preferred_element_type=F32))
        del gate_jobs[:count]

    gate_matmuls(1)

    xprev = xcarry[...]
    for s, rs in zip(st, blocks):
        lx = lru_ref[rs, :lw_]
        xb = row(convb_ref) + convw_ref[CONV_WIDTH - 1:CONV_WIDTH, :] * lx
        for d in range(1, CONV_WIDTH):
            xb = xb + convw_ref[CONV_WIDTH - 1 - d:CONV_WIDTH - d, :] * _shift_rows(xprev, lx, d)
        xprev = lx[MIX_OUT_ROWS - SUBLANES:, :]
        s["xb"] = xb
        s["lgates"] = _dot(xb, wax_ref[...]) + row(bax_ref)
    xcarry[...] = xprev
    gate_matmuls(1)

    for s, rs in zip(st, blocks):
        s["yc"] = s["y"] - s["mu"]
        s["var"] = _dot(s["yc"] * s["yc"], hmean_ref[...])
    gate_matmuls(1)
    for s, rs in zip(st, blocks):
        yn = s["yc"] * lax.rsqrt(s["var"] + GN_EPS) * row(lnw_ref) + row(lnb_ref) + bonus_ref[rs, :]
        s["rwkv_proj"] = _dot(yn * g_ref[rs, :], prw_ref[...])
    gate_matmuls(len(gate_jobs))

    h_prev = hcarry[SUBLANES - 1:SUBLANES, :]
    for i, (s, rs) in enumerate(zip(st, blocks)):
        gate_a = jax.nn.sigmoid(s["lgates"][:, :lw_])
        gate_x = jax.nn.sigmoid(s["lgates"][:, lw_:])
        log_a = -LRU_C * gate_a * _softplus(-row(lam_ref), accurate=True)
        a_l = jnp.exp(log_a)
        mult = jnp.sqrt(-jnp.tanh(log_a) * (a_l * a_l + 1.0))
        if i == 0:
            r_idx = lax.broadcasted_iota(jnp.int32, log_a.shape, 0)
            seq_start = r_idx + jnp.where(first, 0, 1) == 0
            mult = jnp.where(seq_start, 1.0, mult)
        hseq = _scan_rows(a_l, s["xb"] * gate_x * mult, h_prev)
        h_prev = hseq[MIX_OUT_ROWS - 1:, :]
        if i == len(blocks) - 1:
            hcarry[...] = hseq[MIX_OUT_ROWS - SUBLANES:, :]
        s["lru_proj"] = _dot(hseq * jax.nn.gelu(lru_ref[rs, lw_:]), plr_ref[...])

    for s, rs in zip(st, blocks):
        merged = s["gate_r"] * s["rwkv_proj"] + s["gate_l"] * s["lru_proj"]
        out = _dot(merged, wout_ref[...])
        o_ref[rs, :] = s["x"] + _rmsnorm(out, gnorm_ref[MIXER_NORM_ROW + 1:MIXER_NORM_ROW + 2, :])


def _mixer_out(x, seq, y, bonus, g, lru_in, l, norm_g, weights, prm, hmean):
    n = x.shape[0]
    tm = MIX_OUT_TILE
    row512 = pl.BlockSpec((tm, RWKV_WIDTH), lambda i: (i, 0))
    row1024 = pl.BlockSpec((tm, D_MODEL), lambda i: (i, 0))
    names = ["ln_w", "ln_b", "conv_w", "conv_b", "wax", "bax", "lru_lambda"]
    return pl.pallas_call(
        functools.partial(_mixer_out_kernel, l, seq // tm),
        grid=(n // tm,),
        in_specs=([row1024, row512, row512, row512, row1024, _layer(norm_g, l)]
                  + [_resident(w.shape) for w in weights]
                  + [_rows(prm[k]) if prm[k].ndim == 2 else _layer(prm[k], l) for k in names]
                  + [_full((RWKV_WIDTH, RWKV_WIDTH))]),
        out_specs=row1024,
        out_shape=jax.ShapeDtypeStruct((n, D_MODEL), F32),
        scratch_shapes=[
            pltpu.VMEM((SUBLANES, LRU_WIDTH), F32),
            pltpu.VMEM((SUBLANES, LRU_WIDTH), F32),
        ],
        compiler_params=pltpu.CompilerParams(
            dimension_semantics=("arbitrary",), vmem_limit_bytes=VMEM_LIMIT),
        name="mixer_out",
    )(x, y, bonus, g, lru_in, norm_g, *weights, *[prm[k] for k in names], hmean)


def _block_diag(w):
    nl, nb, d, e = w.shape
    eye = jnp.eye(nb, dtype=w.dtype)
    return jnp.einsum("lnde,nm->lndme", w, eye).reshape(nl, nb * d, nb * e)


def kernel(x, norm_g, ffn1_w_up, ffn1_w_down, ffn2_w_up, ffn2_w_down, w_in, shift_mu, w0, w_up, a0, a_up, g_up, k_k, k_a, r_k, ln_w, ln_b, v0, v_down, v_up, conv_w, conv_b, lru_wa, lru_ba, lru_wx, lru_bx, lru_lambda, p_rwkv, p_lru, w_out):
    batch, seq, d = x.shape
    depth = norm_g.shape[0]
    assert d == D_MODEL and seq % MIX_TILE == 0 and (batch * seq) % FFN_TILE == 0
    assert seq % (RWKV_CHUNKS * CHUNK) == 0 and seq % MIX_OUT_TILE == 0
    xf = x.reshape(batch * seq, d)
    bf = lambda t: t.astype(BF16)
    head = jnp.arange(RWKV_WIDTH) // HEAD_DIM
    same = (head[:, None] == head[None, :])
    hsum = same.astype(BF16)
    hmean = (same.astype(F32) / HEAD_DIM).astype(BF16)
    zeros_lora = jnp.zeros((depth, DECAY_LORA, RWKV_WIDTH), F32)
    prm = dict(
        shift_mu=shift_mu, w0=w0, a0=a0, k_k=k_k, k_a=k_a, r_k=r_k,
        wup_pad=bf(jnp.concatenate([w_up, zeros_lora], axis=1)),
        aup_pad=bf(jnp.concatenate([zeros_lora, a_up], axis=1)),
        g_up=bf(g_up), v0=v0, v_down=bf(v_down), v_up=bf(v_up),
        conv_w=conv_w, conv_b=conv_b,
        wax=bf(jnp.concatenate([_block_diag(lru_wa), _block_diag(lru_wx)], axis=2)),
        bax=jnp.concatenate([lru_ba, lru_bx], axis=1), lru_lambda=lru_lambda,
        ln_w=ln_w, ln_b=ln_b,
    )
    ffn1_b = (bf(ffn1_w_up[0]), bf(ffn1_w_down[0]))
    v_first = None
    for l in range(depth):
        xf, (w_in_b, p_rwkv_b, p_lru_b, w_out_b, *ffn2_b) = _ffn(
            xf, l, FFN1_NORM_ROW, norm_g, *ffn1_b,
            to_cast=[w_in, p_rwkv, p_lru, w_out, ffn2_w_up, ffn2_w_down], cast_layer=l)
        r, lw, k, v, kk, a, bonus, gate, lru_in = _mixer_in(xf, seq, l, norm_g, w_in_b, prm, v_first, hsum)
        if l == 0:
            v_first = v
        y = _rwkv(r, lw, k, v, kk, a, batch, seq)
        xf = _mixer_out(xf, seq, y, bonus, gate, lru_in, l, norm_g, (w_in_b, p_rwkv_b, p_lru_b, w_out_b),
                        prm, hmean)
        nxt = [ffn1_w_up, ffn1_w_down] if l + 1 < depth else []
        xf, ffn1_b = _ffn(xf, l, FFN2_NORM_ROW, norm_g, *ffn2_b, to_cast=nxt, cast_layer=l + 1)
    return xf.reshape(batch, seq, d)
```

```python
import functools

import jax
import jax.numpy as jnp
from jax import lax
from jax.experimental import pallas as pl
from jax.experimental.pallas import tpu as pltpu

D_MODEL = 1024
RWKV_WIDTH = 512
HEAD_DIM = 64
LRU_WIDTH = 512
LRU_BLOCKS = 8
CONV_WIDTH = 4
LRU_C = 8.0
D_FF = 2816
RMS_EPS = 1e-6
GN_EPS = 64e-5
FFN1_NORM_ROW = 0
MIXER_NORM_ROW = 2
FFN2_NORM_ROW = 4
DECAY_LORA = 64
AAA_LORA = 64
GATE_LORA = 128
W_SHIFT = 3 * RWKV_WIDTH + DECAY_LORA + AAA_LORA + GATE_LORA
W_MIX = W_SHIFT + 2 * LRU_WIDTH
W_IN = W_MIX + 2 * D_MODEL

RWKV_CHUNKS = 8
CHUNK = 64
LANES = 128
SUBLANES = 8
BF16_ROWS = 16
FFN_TILE = 1024
FFN_COLS = 256
MIX_TILE = 512
MIX_OUT_TILE = 512
MIX_OUT_ROWS = 256
PROJ_COLS = 256
PROJ_SPLIT = (2, 2, 2, 5)
VMEM_LIMIT = 56 * 1024 * 1024

BF16 = jnp.bfloat16
F32 = jnp.float32


def _dot(a, b):
    return jnp.dot(a.astype(BF16), b.astype(BF16), preferred_element_type=F32)


def _dot_nt(a, b):
    return lax.dot_general(a.astype(BF16), b.astype(BF16), (((1,), (1,)), ((), ())),
                           preferred_element_type=F32)


def _dot_tn(a, b):
    return lax.dot_general(a.astype(BF16), b.astype(BF16), (((0,), (0,)), ((), ())),
                           preferred_element_type=F32)


def _rmsnorm(x, g):
    ms = jnp.mean(x * x, axis=-1, keepdims=True)
    return x * lax.rsqrt(ms + RMS_EPS) * g


def _softplus(x, accurate):
    e = jnp.exp(-jnp.abs(x))
    return jnp.maximum(x, 0.0) + (jnp.log1p(e) if accurate else jnp.log(1.0 + e))


def _shift_rows(prev_rows, x, d):
    ext = jnp.concatenate([prev_rows, x], axis=0)
    return pltpu.roll(ext, d, 0)[SUBLANES:]


def _full(shape):
    return pl.BlockSpec(shape, lambda *_: (0,) * len(shape))


def _resident(shape):
    return pl.BlockSpec(shape, lambda *_: (0,) * len(shape), pipeline_mode=pl.Buffered(1))


def _layer(arr, l):
    shape = arr.shape[1:]
    return pl.BlockSpec((None,) + shape, lambda *_: (l,) + (0,) * len(shape),
                        pipeline_mode=pl.Buffered(1))


def _rows(arr):
    return pl.BlockSpec(arr.shape, lambda *_: (0, 0), pipeline_mode=pl.Buffered(1))


def _cast_job(src, l, steps):
    _, rows, cols = src.shape
    blk = next(b for b in range(BF16_ROWS, rows + 1, BF16_ROWS)
               if rows % b == 0 and b * steps >= rows)
    last = rows // blk - 1
    in_spec = pl.BlockSpec((None, blk, cols), lambda i: (l, jnp.minimum(i, last), 0))
    out_spec = pl.BlockSpec((blk, cols), lambda i: (jnp.minimum(i, last), 0))
    return in_spec, out_spec, jax.ShapeDtypeStruct((rows, cols), BF16)


def _ffn_kernel(norm_row, n_cast, x_ref, g_ref, wup_ref, wdn_ref, *refs):
    cast_src, o_ref, cast_dst = refs[:n_cast], refs[n_cast], refs[n_cast + 1:]
    x = x_ref[...]
    h = _rmsnorm(x, g_ref[norm_row:norm_row + 1, :]).astype(BF16)
    acc = jnp.zeros(x.shape, F32)
    for j in range(D_FF // FFN_COLS):
        lo = j * FFN_COLS
        gate = jnp.dot(h, wup_ref[:, lo:lo + FFN_COLS], preferred_element_type=F32)
        up = jnp.dot(h, wup_ref[:, D_FF + lo:D_FF + lo + FFN_COLS], preferred_element_type=F32)
        act = (gate * jax.nn.sigmoid(gate) * up).astype(BF16)
        acc = acc + jnp.dot(act, wdn_ref[lo:lo + FFN_COLS, :], preferred_element_type=F32)
    o_ref[...] = x + 0.5 * _rmsnorm(acc, g_ref[norm_row + 1:norm_row + 2, :])
    for src, dst in zip(cast_src, cast_dst):
        dst[...] = src[...].astype(BF16)


def _ffn(x, l, norm_row, norm_g, w_up, w_down, to_cast, cast_layer):
    n = x.shape[0]
    steps = n // FFN_TILE
    jobs = [_cast_job(src, cast_layer, steps) for src in to_cast]
    tile = pl.BlockSpec((FFN_TILE, D_MODEL), lambda i: (i, 0))
    out = pl.pallas_call(
        functools.partial(_ffn_kernel, norm_row, len(jobs)),
        grid=(steps,),
        in_specs=[tile, _layer(norm_g, l), _resident(w_up.shape), _resident(w_down.shape)]
        + [j[0] for j in jobs],
        out_specs=[tile] + [j[1] for j in jobs],
        out_shape=[jax.ShapeDtypeStruct((n, D_MODEL), F32)] + [j[2] for j in jobs],
        compiler_params=pltpu.CompilerParams(
            dimension_semantics=("arbitrary",), vmem_limit_bytes=VMEM_LIMIT),
        name="ffn",
    )(x, norm_g, w_up, w_down, *to_cast)
    return out[0], out[1:]


def _scan_rows(a, b, h0):
    n, w = a.shape
    groups = n // SUBLANES
    a = a.reshape(groups, SUBLANES, w)
    b = b.reshape(groups, SUBLANES, w)
    sub = lax.broadcasted_iota(jnp.int32, a.shape, 1)
    d = 1
    while d < SUBLANES:
        keep = sub >= d
        a_prev = jnp.where(keep, pltpu.roll(a, d, 1), 1.0)
        b_prev = jnp.where(keep, pltpu.roll(b, d, 1), 0.0)
        b = a * b_prev + b
        a = a * a_prev
        d *= 2
    out = []
    for i in range(groups):
        hi = a[i] * h0 + b[i]
        out.append(hi)
        h0 = hi[SUBLANES - 1:SUBLANES, :]
    return jnp.concatenate(out, axis=0)


def _mixer_in_kernel(layer, has_vmix, tiles_per_seq, *refs):
    (x_ref, g_ref, win_ref, mu_ref, w0_ref, wup_ref, a0_ref, aup_ref, gup_ref,
     kk_ref, ka_ref, rk_ref) = refs[:12]
    refs = refs[12:]
    if has_vmix:
        v0_ref, vdown_ref, vup_ref, vfirst_ref = refs[:4]
        refs = refs[4:]
    (hsum_ref, r_out, lw_out, k_out, v_out, kkn_out, a_out, bonus_out, g_out, lru_out,
     proj_even, proj_odd, pcarry) = refs

    tm = x_ref.shape[0]
    step = pl.program_id(0)
    row = lambda ref, l=layer: ref[l:l + 1, :]
    first = (step + tiles_per_seq - 1) % tiles_per_seq == 0

    @pl.when(step == 0)
    def _():
        proj_odd[...] = jnp.zeros(proj_odd.shape, F32)

    @pl.when(jnp.logical_or(first, step == 0))
    def _():
        pcarry[...] = jnp.zeros(pcarry.shape, F32)

    def body(proj_new, proj):
        h = _rmsnorm(x_ref[...], g_ref[MIXER_NORM_ROW:MIXER_NORM_ROW + 1, :]).astype(BF16)
        chunk_starts = iter(range(0, W_MIX, PROJ_COLS))

        def project(n_chunks):
            for _ in range(n_chunks):
                lo = next(chunk_starts)
                proj_new[:, lo:lo + PROJ_COLS] = jnp.dot(
                    h, win_ref[:, lo:lo + PROJ_COLS], preferred_element_type=F32)

        project(PROJ_SPLIT[0])
        p = proj[:, :W_SHIFT]
        prev = _shift_rows(pcarry[...], p, 1)
        pcarry[...] = p[tm - SUBLANES:, :]
        p = p + row(mu_ref) * (prev - p)
        rw = RWKV_WIDTH
        r, k, v = p[:, 0:rw], p[:, rw:2 * rw], p[:, 2 * rw:3 * rw]
        xwa = p[:, 3 * rw:3 * rw + DECAY_LORA + AAA_LORA]
        xg = p[:, 3 * rw + DECAY_LORA + AAA_LORA:]
        w = -_softplus(-(row(w0_ref) + _dot(jnp.tanh(xwa), wup_ref[...])), accurate=False) - 0.5
        a = jax.nn.sigmoid(row(a0_ref) + _dot(xwa, aup_ref[...]))
        g_out[...] = _dot(jax.nn.sigmoid(xg), gup_ref[...])
        if has_vmix:
            v_lora = _dot(v, vdown_ref[...])
        project(PROJ_SPLIT[1])
        lw_out[...] = -jnp.exp(w)
        if has_vmix:
            mix = jax.nn.sigmoid(row(v0_ref, layer - 1) + _dot(v_lora, vup_ref[...]))
            v = v + (vfirst_ref[...] - v) * mix
        kk = k * row(kk_ref)
        kk = kk * lax.rsqrt(jnp.maximum(_dot(kk * kk, hsum_ref[...]), 1e-24))
        project(PROJ_SPLIT[2])
        k = k * (1.0 + (a - 1.0) * row(ka_ref))
        r_out[...] = r
        k_out[...] = k
        v_out[...] = v
        kkn_out[...] = kk
        a_out[...] = a
        bonus_out[...] = _dot(r * k * row(rk_ref), hsum_ref[...]) * v
        project(PROJ_SPLIT[3])
        lru_out[...] = proj[:, W_SHIFT:]

    @pl.when(step % 2 == 0)
    def _():
        body(proj_even, proj_odd)

    @pl.when(step % 2 == 1)
    def _():
        body(proj_odd, proj_even)


def _mixer_in(x, seq, l, norm_g, w_in_b, prm, v_first, hsum):
    n = x.shape[0]
    tm = MIX_TILE
    tiles = n // tm
    has_vmix = v_first is not None
    x_spec = pl.BlockSpec((tm, D_MODEL), lambda i: (jnp.minimum(i, tiles - 1), 0))
    lag512 = pl.BlockSpec((tm, RWKV_WIDTH), lambda i: (jnp.maximum(i - 1, 0), 0))
    names = ["shift_mu", "w0", "wup_pad", "a0", "aup_pad", "g_up", "k_k", "k_a", "r_k"]
    spec = lambda k, layer: _rows(prm[k]) if prm[k].ndim == 2 else _layer(prm[k], layer)
    w_mix_spec = pl.BlockSpec((D_MODEL, W_MIX), lambda i: (0, 0), pipeline_mode=pl.Buffered(1))
    in_specs = [x_spec, _layer(norm_g, l), w_mix_spec] + [spec(k, l) for k in names]
    args = [x, norm_g, w_in_b] + [prm[k] for k in names]
    if has_vmix:
        vnames = ["v0", "v_down", "v_up"]
        in_specs += [spec(k, l - 1) for k in vnames] + [lag512]
        args += [prm[k] for k in vnames] + [v_first]
    in_specs += [_full((RWKV_WIDTH, RWKV_WIDTH))]
    args += [hsum]
    out512 = jax.ShapeDtypeStruct((n, RWKV_WIDTH), F32)
    lag_lru = pl.BlockSpec((tm, 2 * LRU_WIDTH), lambda i: (jnp.maximum(i - 1, 0), 0))
    return pl.pallas_call(
        functools.partial(_mixer_in_kernel, l, has_vmix, seq // tm),
        grid=(tiles + 1,),
        in_specs=in_specs,
        out_specs=[lag512] * 8 + [lag_lru],
        out_shape=[out512] * 8 + [jax.ShapeDtypeStruct((n, 2 * LRU_WIDTH), F32)],
        scratch_shapes=[
            pltpu.VMEM((tm, W_MIX), F32),
            pltpu.VMEM((tm, W_MIX), F32),
            pltpu.VMEM((SUBLANES, W_SHIFT), F32),
        ],
        compiler_params=pltpu.CompilerParams(
            dimension_semantics=("arbitrary",), vmem_limit_bytes=VMEM_LIMIT),
        name="mixer_in",
    )(*args)


def _cumsum_rows(x):
    n, w = x.shape
    groups = n // SUBLANES
    x = x.reshape(groups, SUBLANES, w)
    sub = lax.broadcasted_iota(jnp.int32, x.shape, 1)
    d = 1
    while d < SUBLANES:
        x = x + jnp.where(sub >= d, pltpu.roll(x, d, 1), 0.0)
        d *= 2
    out = [x[0]]
    for i in range(1, groups):
        out.append(x[i] + out[-1][SUBLANES - 1:SUBLANES, :])
    return jnp.concatenate(out, axis=0)


def _rwkv_kernel(r_ref, lw_ref, k_ref, v_ref, kk_ref, a_ref, y_ref, s_ref):
    c = CHUNK
    nb = r_ref.shape[0]
    pairs = RWKV_WIDTH // LANES
    chains = [(bi, p) for bi in range(nb) for p in range(pairs)]

    @pl.when(pl.program_id(0) == 0)
    def _():
        s_ref[...] = jnp.zeros(s_ref.shape, F32)

    lane = lax.broadcasted_iota(jnp.int32, (c, LANES), 1)
    t_idx = lax.broadcasted_iota(jnp.int32, (c, LANES), 0)
    left = lane < HEAD_DIM
    s_idx = jnp.where(left, lane, lane - HEAD_DIM)
    strict = t_idx > s_idx
    incl = t_idx >= s_idx
    eye_w = (t_idx == s_idx).astype(F32)
    row2 = lax.broadcasted_iota(jnp.int32, (LANES, LANES), 0)
    lane2 = lax.broadcasted_iota(jnp.int32, (LANES, LANES), 1)
    same_head = (row2 < HEAD_DIM) == (lane2 < HEAD_DIM)
    eye2 = row2 == lane2

    def bd(w):
        return jnp.concatenate([jnp.where(left, w, 0.0), jnp.where(left, 0.0, w)], axis=0)

    def prep(g):
        ops = {}
        rows = slice(g * c, (g + 1) * c)
        for bi in range(nb):
            lw = lw_ref[bi, rows, :]
            cum = _cumsum_rows(lw)
            last = cum[c - 1:c, :]
            kk = kk_ref[bi, rows, :]
            k = k_ref[bi, rows, :]
            b = kk * a_ref[bi, rows, :]
            e_neg = jnp.exp(-cum)
            e_tail = jnp.exp(last - cum)
            a_t = -kk * jnp.exp(cum - lw)
            r_t = r_ref[bi, rows, :] * jnp.exp(cum)
            b_t = b * e_neg
            k_t = k * e_neg
            b_h = b * e_tail
            k_h = k * e_tail
            w_c = jnp.exp(last)
            v = v_ref[bi, rows, :]
            for p in range(pairs):
                sl = slice(p * LANES, (p + 1) * LANES)
                w_col = jnp.sum(jnp.where(eye2, w_c[:, sl], 0.0), axis=1, keepdims=True)
                ops[bi, p] = dict(a=a_t[:, sl], r=r_t[:, sl], bt=b_t[:, sl], kt=k_t[:, sl],
                                  bh=b_h[:, sl], kh=k_h[:, sl], v=v[:, sl], w_col=w_col)
        for ch in chains:
            o = ops[ch]
            m = _dot_nt(jnp.concatenate([o["a"], o["r"]], axis=0),
                        jnp.concatenate([bd(o["bt"]), bd(o["kt"])], axis=0))
            o["a_ab"] = jnp.where(strict, m[:c, :LANES], 0.0)
            o["a_ak"] = jnp.where(strict, m[:c, LANES:], 0.0)
            o["a_rb"] = jnp.where(incl, m[c:, :LANES], 0.0)
            o["a_rk"] = jnp.where(incl, m[c:, LANES:], 0.0)
        return ops

    def inverse_stages(ops):
        def first():
            for ch in chains:
                o = ops[ch]
                o["t"] = eye_w + o["a_ab"]
                o["p"] = _dot(o["a_ab"], bd(o["a_ab"]))

        def double(is_last):
            def run():
                for ch in chains:
                    o = ops[ch]
                    p_bd = bd(o["p"])
                    if is_last:
                        o["t"] = o["t"] + _dot(o["t"], p_bd)
                    else:
                        prod = _dot(jnp.concatenate([o["t"], o["p"]], axis=0), p_bd)
                        o["t"] = o["t"] + prod[:c]
                        o["p"] = prod[c:]
            return run

        steps = CHUNK.bit_length() - 2
        return [first] + [double(i == steps - 1) for i in range(steps)]

    def state_stages(ops, state, g):
        rows = slice(g * c, (g + 1) * c)

        def stage_x():
            for ch in chains:
                o = ops[ch]
                o["v_bd"] = bd(o["v"])
                o["x"] = _dot(jnp.concatenate([o["a"], o["a_ak"]], axis=1),
                              jnp.concatenate([state[ch], o["v_bd"]], axis=0))

        def stage_u():
            for ch in chains:
                o = ops[ch]
                o["u"] = _dot(o["t"], bd(o["x"]))

        def stage_s():
            for ch in chains:
                o = ops[ch]
                ds = _dot_tn(jnp.concatenate([o["bh"], o["kh"]], axis=0),
                             jnp.concatenate([o["u"], o["v"]], axis=0))
                o["s_new"] = o["w_col"] * state[ch] + jnp.where(same_head, ds, 0.0)

        def stage_y():
            for ch in chains:
                o = ops[ch]
                bi, p = ch
                y_ref[bi, rows, p * LANES:(p + 1) * LANES] = _dot(
                    jnp.concatenate([o["r"], o["a_rb"], o["a_rk"]], axis=1),
                    jnp.concatenate([state[ch], bd(o["u"]), o["v_bd"]], axis=0))
                state[ch] = o["s_new"]

        return [stage_x, stage_u, stage_s, stage_y]

    state = {ch: s_ref[ch[0], ch[1]] for ch in chains}
    ops_next = prep(0)
    pending = []
    for g in range(RWKV_CHUNKS):
        ops_cur = ops_next
        inv = inverse_stages(ops_cur)
        half = len(inv) // 2
        for i, stage in enumerate(inv):
            stage()
            if i < len(pending):
                pending[i]()
            if i == half and g + 1 < RWKV_CHUNKS:
                ops_next = prep(g + 1)
        for stage in pending[len(inv):]:
            stage()
        pending = state_stages(ops_cur, state, g)
    for stage in pending:
        stage()
    for ch, s_val in state.items():
        s_ref[ch[0], ch[1]] = s_val


def _rwkv(r, lw, k, v, kk, a, batch, seq):
    rows = RWKV_CHUNKS * CHUNK
    spec = pl.BlockSpec((batch, rows, RWKV_WIDTH), lambda i: (0, i, 0))
    shape3 = (batch, seq, RWKV_WIDTH)
    y = pl.pallas_call(
        _rwkv_kernel,
        grid=(seq // rows,),
        in_specs=[spec] * 6,
        out_specs=spec,
        out_shape=jax.ShapeDtypeStruct(shape3, F32),
        scratch_shapes=[pltpu.VMEM((batch, RWKV_WIDTH // LANES, LANES, LANES), F32)],
        compiler_params=pltpu.CompilerParams(
            dimension_semantics=("arbitrary",), vmem_limit_bytes=VMEM_LIMIT),
        name="rwkv",
    )(*(t.reshape(shape3) for t in (r, lw, k, v, kk, a)))
    return y.reshape(batch * seq, RWKV_WIDTH)


def _mixer_out_kernel(layer, tiles_per_seq, x_ref, y_ref, bonus_ref, g_ref, lru_ref, gnorm_ref, win_ref, prw_ref,
                      plr_ref, wout_ref, lnw_ref, lnb_ref, convw_ref, convb_ref, wax_ref, bax_ref, lam_ref,
                      hmean_ref, o_ref, xcarry, hcarry):
    tm = x_ref.shape[0]
    first = pl.program_id(0) % tiles_per_seq == 0
    row = lambda ref: ref[layer:layer + 1, :]

    @pl.when(first)
    def _():
        xcarry[...] = jnp.zeros(xcarry.shape, F32)
        hcarry[...] = jnp.zeros(hcarry.shape, F32)

    blocks = [slice(i * MIX_OUT_ROWS, (i + 1) * MIX_OUT_ROWS) for i in range(tm // MIX_OUT_ROWS)]
    st = [dict() for _ in blocks]
    lw_ = LRU_WIDTH
    for s, rs in zip(st, blocks):
        s["x"] = x_ref[rs, :]
        s["hx"] = _rmsnorm(s["x"], gnorm_ref[MIXER_NORM_ROW:MIXER_NORM_ROW + 1, :]).astype(BF16)
        s["y"] = y_ref[rs, :]
        s["mu"] = _dot(s["y"], hmean_ref[...])

    gate_jobs = [(s, name, W_MIX + j * D_MODEL) for s in st for j, name in enumerate(("gate_r", "gate_l"))]

    def gate_matmuls(count):
        for s, name, lo in gate_jobs[:count]:
            s[name] = jax.nn.sigmoid(jnp.dot(s["hx"], win_ref[:, lo:lo + D_MODEL], preferred_element_type=F32))
        del gate_jobs[:count]

    gate_matmuls(1)

    xprev = xcarry[...]
    for s, rs in zip(st, blocks):
        lx = lru_ref[rs, :lw_]
        xb = row(convb_ref) + convw_ref[CONV_WIDTH - 1:CONV_WIDTH, :] * lx
        for d in range(1, CONV_WIDTH):
            xb = xb + convw_ref[CONV_WIDTH - 1 - d:CONV_WIDTH - d, :] * _shift_rows(xprev, lx, d)
        xprev = lx[MIX_OUT_ROWS - SUBLANES:, :]
        s["xb"] = xb
        s["lgates"] = _dot(xb, wax_ref[...]) + row(bax_ref)
    xcarry[...] = xprev
    gate_matmuls(1)

    for s, rs in zip(st, blocks):
        s["yc"] = s["y"] - s["mu"]
        s["var"] = _dot(s["yc"] * s["yc"], hmean_ref[...])
    gate_matmuls(1)
    for s, rs in zip(st, blocks):
        yn = s["yc"] * lax.rsqrt(s["var"] + GN_EPS) * row(lnw_ref) + row(lnb_ref) + bonus_ref[rs, :]
        s["rwkv_proj"] = _dot(yn * g_ref[rs, :], prw_ref[...])
    gate_matmuls(len(gate_jobs))

    h_prev = hcarry[SUBLANES - 1:SUBLANES, :]
    for i, (s, rs) in enumerate(zip(st, blocks)):
        gate_a = jax.nn.sigmoid(s["lgates"][:, :lw_])
        gate_x = jax.nn.sigmoid(s["lgates"][:, lw_:])
        log_a = -LRU_C * gate_a * _softplus(-row(lam_ref), accurate=True)
        a_l = jnp.exp(log_a)
        mult = jnp.sqrt(-jnp.tanh(log_a) * (a_l * a_l + 1.0))
        if i == 0:
            r_idx = lax.broadcasted_iota(jnp.int32, log_a.shape, 0)
            seq_start = r_idx + jnp.where(first, 0, 1) == 0
            mult = jnp.where(seq_start, 1.0, mult)
        hseq = _scan_rows(a_l, s["xb"] * gate_x * mult, h_prev)
        h_prev = hseq[MIX_OUT_ROWS - 1:, :]
        if i == len(blocks) - 1:
            hcarry[...] = hseq[MIX_OUT_ROWS - SUBLANES:, :]
        s["lru_proj"] = _dot(hseq * jax.nn.gelu(lru_ref[rs, lw_:]), plr_ref[...])

    for s, rs in zip(st, blocks):
        merged = s["gate_r"] * s["rwkv_proj"] + s["gate_l"] * s["lru_proj"]
        out = _dot(merged, wout_ref[...])
        o_ref[rs, :] = s["x"] + _rmsnorm(out, gnorm_ref[MIXER_NORM_ROW + 1:MIXER_NORM_ROW + 2, :])


def _mixer_out(x, seq, y, bonus, g, lru_in, l, norm_g, weights, prm, hmean):
    n = x.shape[0]
    tm = MIX_OUT_TILE
    row512 = pl.BlockSpec((tm, RWKV_WIDTH), lambda i: (i, 0))
    row1024 = pl.BlockSpec((tm, D_MODEL), lambda i: (i, 0))
    names = ["ln_w", "ln_b", "conv_w", "conv_b", "wax", "bax", "lru_lambda"]
    return pl.pallas_call(
        functools.partial(_mixer_out_kernel, l, seq // tm),
        grid=(n // tm,),
        in_specs=([row1024, row512, row512, row512, row1024, _layer(norm_g, l)]
                  + [_resident(w.shape) for w in weights]
                  + [_rows(prm[k]) if prm[k].ndim == 2 else _layer(prm[k], l) for k in names]
                  + [_full((RWKV_WIDTH, RWKV_WIDTH))]),
        out_specs=row1024,
        out_shape=jax.ShapeDtypeStruct((n, D_MODEL), F32),
        scratch_shapes=[
            pltpu.VMEM((SUBLANES, LRU_WIDTH), F32),
            pltpu.VMEM((SUBLANES, LRU_WIDTH), F32),
        ],
        compiler_params=pltpu.CompilerParams(
            dimension_semantics=("arbitrary",), vmem_limit_bytes=VMEM_LIMIT),
        name="mixer_out",
    )(x, y, bonus, g, lru_in, norm_g, *weights, *[prm[k] for k in names], hmean)


def _block_diag(w):
    nl, nb, d, e = w.shape
    eye = jnp.eye(nb, dtype=w.dtype)
    return jnp.einsum("lnde,nm->lndme", w, eye).reshape(nl, nb * d, nb * e)


def kernel(x, norm_g, ffn1_w_up, ffn1_w_down, ffn2_w_up, ffn2_w_down, w_in, shift_mu, w0, w_up, a0, a_up, g_up, k_k, k_a, r_k, ln_w, ln_b, v0, v_down, v_up, conv_w, conv_b, lru_wa, lru_ba, lru_wx, lru_bx, lru_lambda, p_rwkv, p_lru, w_out):
    batch, seq, d = x.shape
    depth = norm_g.shape[0]
    assert d == D_MODEL and seq % MIX_TILE == 0 and (batch * seq) % FFN_TILE == 0
    assert seq % (RWKV_CHUNKS * CHUNK) == 0 and seq % MIX_OUT_TILE == 0
    xf = x.reshape(batch * seq, d)
    bf = lambda t: t.astype(BF16)
    head = jnp.arange(RWKV_WIDTH) // HEAD_DIM
    same = (head[:, None] == head[None, :])
    hsum = same.astype(BF16)
    hmean = (same.astype(F32) / HEAD_DIM).astype(BF16)
    zeros_lora = jnp.zeros((depth, DECAY_LORA, RWKV_WIDTH), F32)
    prm = dict(
        shift_mu=shift_mu, w0=w0, a0=a0, k_k=k_k, k_a=k_a, r_k=r_k,
        wup_pad=bf(jnp.concatenate([w_up, zeros_lora], axis=1)),
        aup_pad=bf(jnp.concatenate([zeros_lora, a_up], axis=1)),
        g_up=bf(g_up), v0=v0, v_down=bf(v_down), v_up=bf(v_up),
        conv_w=conv_w, conv_b=conv_b,
        wax=bf(jnp.concatenate([_block_diag(lru_wa), _block_diag(lru_wx)], axis=2)),
        bax=jnp.concatenate([lru_ba, lru_bx], axis=1), lru_lambda=lru_lambda,
        ln_w=ln_w, ln_b=ln_b,
    )
    ffn1_b = (bf(ffn1_w_up[0]), bf(ffn1_w_down[0]))
    v_first = None
    for l in range(depth):
        xf, (w_in_b, p_rwkv_b, p_lru_b, w_out_b, *ffn2_b) = _ffn(
            xf, l, FFN1_NORM_ROW, norm_g, *ffn1_b,
            to_cast=[w_in, p_rwkv, p_lru, w_out, ffn2_w_up, ffn2_w_down], cast_layer=l)
        r, lw, k, v, kk, a, bonus, gate, lru_in = _mixer_in(xf, seq, l, norm_g, w_in_b, prm, v_first, hsum)
        if l == 0:
            v_first = v
        y = _rwkv(r, lw, k, v, kk, a, batch, seq)
        xf = _mixer_out(xf, seq, y, bonus, gate, lru_in, l, norm_g, (w_in_b, p_rwkv_b, p_lru_b, w_out_b),
                        prm, hmean)
        nxt = [ffn1_w_up, ffn1_w_down] if l + 1 < depth else []
        xf, ffn1_b = _ffn(xf, l, FFN2_NORM_ROW, norm_g, *ffn2_b, to_cast=nxt, cast_layer=l + 1)
    return xf.reshape(batch, seq, d)
```

```python
import functools

import jax
import jax.numpy as jnp
from jax import lax
from jax.experimental import pallas as pl
from jax.experimental.pallas import tpu as pltpu

D_MODEL = 1024
RWKV_WIDTH = 512
HEAD_DIM = 64
LRU_WIDTH = 512
LRU_BLOCKS = 8
CONV_WIDTH = 4
LRU_C = 8.0
D_FF = 2816
RMS_EPS = 1e-6
GN_EPS = 64e-5
FFN1_NORM_ROW = 0
MIXER_NORM_ROW = 2
FFN2_NORM_ROW = 4
DECAY_LORA = 64
AAA_LORA = 64
GATE_LORA = 128
W_SHIFT = 3 * RWKV_WIDTH + DECAY_LORA + AAA_LORA + GATE_LORA
W_MIX = W_SHIFT + 2 * LRU_WIDTH
W_IN = W_MIX + 2 * D_MODEL

RWKV_CHUNKS = 8
CHUNK = 64
LANES = 128
MXU_COLS = 256
SUBLANES = 8
BF16_ROWS = 16
FFN_TILE = 1024
FFN_COLS = 256
MIX_TILE = 512
MIX_OUT_TILE = 512
MIX_OUT_ROWS = 256
PROJ_COLS = 256
PROJ_SPLIT = (2, 2, 2, 5)
VMEM_LIMIT = 56 * 1024 * 1024

BF16 = jnp.bfloat16
F32 = jnp.float32


def _dot(a, b):
    return jnp.dot(a.astype(BF16), b.astype(BF16), preferred_element_type=F32)


def _dot_nt(a, b):
    return lax.dot_general(a.astype(BF16), b.astype(BF16), (((1,), (1,)), ((), ())),
                           preferred_element_type=F32)


def _dot_tn(a, b):
    return lax.dot_general(a.astype(BF16), b.astype(BF16), (((0,), (0,)), ((), ())),
                           preferred_element_type=F32)


def _head_dot(z, h_ref):
    w = h_ref[...]
    return jnp.concatenate(
        [_dot(z[:, i:i + MXU_COLS], w) for i in range(0, z.shape[1], MXU_COLS)], axis=1)


def _rmsnorm(x, g):
    ms = jnp.mean(x * x, axis=-1, keepdims=True)
    return x * lax.rsqrt(ms + RMS_EPS) * g


def _softplus(x, accurate):
    e = jnp.exp(-jnp.abs(x))
    return jnp.maximum(x, 0.0) + (jnp.log1p(e) if accurate else jnp.log(1.0 + e))


def _shift_rows(prev_rows, x, d):
    ext = jnp.concatenate([prev_rows, x], axis=0)
    return pltpu.roll(ext, d, 0)[SUBLANES:]


def _full(shape):
    return pl.BlockSpec(shape, lambda *_: (0,) * len(shape))


def _resident(shape):
    return pl.BlockSpec(shape, lambda *_: (0,) * len(shape), pipeline_mode=pl.Buffered(1))


def _layer(arr, l):
    shape = arr.shape[1:]
    return pl.BlockSpec((None,) + shape, lambda *_: (l,) + (0,) * len(shape),
                        pipeline_mode=pl.Buffered(1))


def _rows(arr):
    return pl.BlockSpec(arr.shape, lambda *_: (0, 0), pipeline_mode=pl.Buffered(1))


def _cast_job(src, l, steps):
    _, rows, cols = src.shape
    blk = next(b for b in range(BF16_ROWS, rows + 1, BF16_ROWS)
               if rows % b == 0 and b * steps >= rows)
    last = rows // blk - 1
    in_spec = pl.BlockSpec((None, blk, cols), lambda i: (l, jnp.minimum(i, last), 0))
    out_spec = pl.BlockSpec((blk, cols), lambda i: (jnp.minimum(i, last), 0))
    return in_spec, out_spec, jax.ShapeDtypeStruct((rows, cols), BF16)


def _ffn_kernel(norm_row, n_cast, x_ref, g_ref, wup_ref, wdn_ref, *refs):
    cast_src, o_ref, cast_dst = refs[:n_cast], refs[n_cast], refs[n_cast + 1:]
    x = x_ref[...]
    h = _rmsnorm(x, g_ref[norm_row:norm_row + 1, :]).astype(BF16)
    acc = jnp.zeros(x.shape, F32)
    for j in range(D_FF // FFN_COLS):
        lo = j * FFN_COLS
        gate = jnp.dot(h, wup_ref[:, lo:lo + FFN_COLS], preferred_element_type=F32)
        up = jnp.dot(h, wup_ref[:, D_FF + lo:D_FF + lo + FFN_COLS], preferred_element_type=F32)
        act = (gate * jax.nn.sigmoid(gate) * up).astype(BF16)
        acc = acc + jnp.dot(act, wdn_ref[lo:lo + FFN_COLS, :], preferred_element_type=F32)
    o_ref[...] = x + 0.5 * _rmsnorm(acc, g_ref[norm_row + 1:norm_row + 2, :])
    for src, dst in zip(cast_src, cast_dst):
        dst[...] = src[...].astype(BF16)


def _ffn(x, l, norm_row, norm_g, w_up, w_down, to_cast, cast_layer):
    n = x.shape[0]
    steps = n // FFN_TILE
    jobs = [_cast_job(src, cast_layer, steps) for src in to_cast]
    tile = pl.BlockSpec((FFN_TILE, D_MODEL), lambda i: (i, 0))
    out = pl.pallas_call(
        functools.partial(_ffn_kernel, norm_row, len(jobs)),
        grid=(steps,),
        in_specs=[tile, _layer(norm_g, l), _resident(w_up.shape), _resident(w_down.shape)]
        + [j[0] for j in jobs],
        out_specs=[tile] + [j[1] for j in jobs],
        out_shape=[jax.ShapeDtypeStruct((n, D_MODEL), F32)] + [j[2] for j in jobs],
        compiler_params=pltpu.CompilerParams(
            dimension_semantics=("arbitrary",), vmem_limit_bytes=VMEM_LIMIT),
        name="ffn",
    )(x, norm_g, w_up, w_down, *to_cast)
    return out[0], out[1:]


def _scan_rows(a, b, h0):
    n, w = a.shape
    groups = n // SUBLANES
    a = a.reshape(groups, SUBLANES, w)
    b = b.reshape(groups, SUBLANES, w)
    sub = lax.broadcasted_iota(jnp.int32, a.shape, 1)
    d = 1
    while d < SUBLANES:
        keep = sub >= d
        a_prev = jnp.where(keep, pltpu.roll(a, d, 1), 1.0)
        b_prev = jnp.where(keep, pltpu.roll(b, d, 1), 0.0)
        b = a * b_prev + b
        a = a * a_prev
        d *= 2
    out = []
    for i in range(groups):
        hi = a[i] * h0 + b[i]
        out.append(hi)
        h0 = hi[SUBLANES - 1:SUBLANES, :]
    return jnp.concatenate(out, axis=0)


def _mixer_in_kernel(layer, has_vmix, tiles_per_seq, *refs):
    (x_ref, g_ref, win_ref, mu_ref, w0_ref, wup_ref, a0_ref, aup_ref, gup_ref,
     kk_ref, ka_ref, rk_ref) = refs[:12]
    refs = refs[12:]
    if has_vmix:
        v0_ref, vdown_ref, vup_ref, vfirst_ref = refs[:4]
        refs = refs[4:]
    (hsum_ref, r_out, lw_out, k_out, v_out, kkn_out, a_out, bonus_out, g_out, lru_out,
     proj_even, proj_odd, pcarry) = refs

    tm = x_ref.shape[0]
    step = pl.program_id(0)
    row = lambda ref, l=layer: ref[l:l + 1, :]
    first = (step + tiles_per_seq - 1) % tiles_per_seq == 0

    @pl.when(step == 0)
    def _():
        proj_odd[...] = jnp.zeros(proj_odd.shape, F32)

    @pl.when(jnp.logical_or(first, step == 0))
    def _():
        pcarry[...] = jnp.zeros(pcarry.shape, F32)

    def body(proj_new, proj):
        h = _rmsnorm(x_ref[...], g_ref[MIXER_NORM_ROW:MIXER_NORM_ROW + 1, :]).astype(BF16)
        chunk_starts = iter(range(0, W_MIX, PROJ_COLS))

        def project(n_chunks):
            for _ in range(n_chunks):
                lo = next(chunk_starts)
                proj_new[:, lo:lo + PROJ_COLS] = jnp.dot(
                    h, win_ref[:, lo:lo + PROJ_COLS], preferred_element_type=F32)

        project(PROJ_SPLIT[0])
        p = proj[:, :W_SHIFT]
        prev = _shift_rows(pcarry[...], p, 1)
        pcarry[...] = p[tm - SUBLANES:, :]
        p = p + row(mu_ref) * (prev - p)
        rw = RWKV_WIDTH
        r, k, v = p[:, 0:rw], p[:, rw:2 * rw], p[:, 2 * rw:3 * rw]
        xwa = p[:, 3 * rw:3 * rw + DECAY_LORA + AAA_LORA]
        xg = p[:, 3 * rw + DECAY_LORA + AAA_LORA:]
        w = -_softplus(-(row(w0_ref) + _dot(jnp.tanh(xwa), wup_ref[...])), accurate=False) - 0.5
        a = jax.nn.sigmoid(row(a0_ref) + _dot(xwa, aup_ref[...]))
        g_out[...] = _dot(jax.nn.sigmoid(xg), gup_ref[...])
        if has_vmix:
            v_lora = _dot(v, vdown_ref[...])
        project(PROJ_SPLIT[1])
        lw_out[...] = -jnp.exp(w)
        if has_vmix:
            mix = jax.nn.sigmoid(row(v0_ref, layer - 1) + _dot(v_lora, vup_ref[...]))
            v = v + (vfirst_ref[...] - v) * mix
        kk = k * row(kk_ref)
        kk = kk * lax.rsqrt(jnp.maximum(_head_dot(kk * kk, hsum_ref), 1e-24))
        project(PROJ_SPLIT[2])
        k = k * (1.0 + (a - 1.0) * row(ka_ref))
        r_out[...] = r
        k_out[...] = k
        v_out[...] = v
        kkn_out[...] = kk
        a_out[...] = a
        bonus_out[...] = _head_dot(r * k * row(rk_ref), hsum_ref) * v
        project(PROJ_SPLIT[3])
        lru_out[...] = proj[:, W_SHIFT:]

    @pl.when(step % 2 == 0)
    def _():
        body(proj_even, proj_odd)

    @pl.when(step % 2 == 1)
    def _():
        body(proj_odd, proj_even)


def _mixer_in(x, seq, l, norm_g, w_in_b, prm, v_first, hsum):
    n = x.shape[0]
    tm = MIX_TILE
    tiles = n // tm
    has_vmix = v_first is not None
    x_spec = pl.BlockSpec((tm, D_MODEL), lambda i: (jnp.minimum(i, tiles - 1), 0))
    lag512 = pl.BlockSpec((tm, RWKV_WIDTH), lambda i: (jnp.maximum(i - 1, 0), 0))
    names = ["shift_mu", "w0", "wup_pad", "a0", "aup_pad", "g_up", "k_k", "k_a", "r_k"]
    spec = lambda k, layer: _rows(prm[k]) if prm[k].ndim == 2 else _layer(prm[k], layer)
    w_mix_spec = pl.BlockSpec((D_MODEL, W_MIX), lambda i: (0, 0), pipeline_mode=pl.Buffered(1))
    in_specs = [x_spec, _layer(norm_g, l), w_mix_spec] + [spec(k, l) for k in names]
    args = [x, norm_g, w_in_b] + [prm[k] for k in names]
    if has_vmix:
        vnames = ["v0", "v_down", "v_up"]
        in_specs += [spec(k, l - 1) for k in vnames] + [lag512]
        args += [prm[k] for k in vnames] + [v_first]
    in_specs += [_full(hsum.shape)]
    args += [hsum]
    out512 = jax.ShapeDtypeStruct((n, RWKV_WIDTH), F32)
    lag_lru = pl.BlockSpec((tm, 2 * LRU_WIDTH), lambda i: (jnp.maximum(i - 1, 0), 0))
    return pl.pallas_call(
        functools.partial(_mixer_in_kernel, l, has_vmix, seq // tm),
        grid=(tiles + 1,),
        in_specs=in_specs,
        out_specs=[lag512] * 8 + [lag_lru],
        out_shape=[out512] * 8 + [jax.ShapeDtypeStruct((n, 2 * LRU_WIDTH), F32)],
        scratch_shapes=[
            pltpu.VMEM((tm, W_MIX), F32),
            pltpu.VMEM((tm, W_MIX), F32),
            pltpu.VMEM((SUBLANES, W_SHIFT), F32),
        ],
        compiler_params=pltpu.CompilerParams(
            dimension_semantics=("arbitrary",), vmem_limit_bytes=VMEM_LIMIT),
        name="mixer_in",
    )(*args)


def _cumsum_rows(x):
    n, w = x.shape
    groups = n // SUBLANES
    x = x.reshape(groups, SUBLANES, w)
    sub = lax.broadcasted_iota(jnp.int32, x.shape, 1)
    d = 1
    while d < SUBLANES:
        x = x + jnp.where(sub >= d, pltpu.roll(x, d, 1), 0.0)
        d *= 2
    out = [x[0]]
    for i in range(1, groups):
        out.append(x[i] + out[-1][SUBLANES - 1:SUBLANES, :])
    return jnp.concatenate(out, axis=0)


def _rwkv_kernel(r_ref, lw_ref, k_ref, v_ref, kk_ref, a_ref, y_ref, s_ref):
    c = CHUNK
    nb = r_ref.shape[0]
    pairs = RWKV_WIDTH // LANES
    chains = [(bi, p) for bi in range(nb) for p in range(pairs)]

    @pl.when(pl.program_id(0) == 0)
    def _():
        s_ref[...] = jnp.zeros(s_ref.shape, F32)

    lane = lax.broadcasted_iota(jnp.int32, (c, LANES), 1)
    t_idx = lax.broadcasted_iota(jnp.int32, (c, LANES), 0)
    left = lane < HEAD_DIM
    s_idx = jnp.where(left, lane, lane - HEAD_DIM)
    strict = t_idx > s_idx
    incl = t_idx >= s_idx
    eye_w = (t_idx == s_idx).astype(F32)
    row2 = lax.broadcasted_iota(jnp.int32, (LANES, LANES), 0)
    lane2 = lax.broadcasted_iota(jnp.int32, (LANES, LANES), 1)
    same_head = (row2 < HEAD_DIM) == (lane2 < HEAD_DIM)
    eye2 = row2 == lane2

    def bd(w):
        return jnp.concatenate([jnp.where(left, w, 0.0), jnp.where(left, 0.0, w)], axis=0)

    def prep(g):
        ops = {}
        rows = slice(g * c, (g + 1) * c)
        for bi in range(nb):
            lw = lw_ref[bi, rows, :]
            cum = _cumsum_rows(lw)
            last = cum[c - 1:c, :]
            kk = kk_ref[bi, rows, :]
            k = k_ref[bi, rows, :]
            b = kk * a_ref[bi, rows, :]
            e_neg = jnp.exp(-cum)
            e_tail = jnp.exp(last - cum)
            a_t = -kk * jnp.exp(cum - lw)
            r_t = r_ref[bi, rows, :] * jnp.exp(cum)
            b_t = b * e_neg
            k_t = k * e_neg
            b_h = b * e_tail
            k_h = k * e_tail
            w_c = jnp.exp(last)
            v = v_ref[bi, rows, :]
            for p in range(pairs):
                sl = slice(p * LANES, (p + 1) * LANES)
                w_col = jnp.sum(jnp.where(eye2, w_c[:, sl], 0.0), axis=1, keepdims=True)
                ops[bi, p] = dict(a=a_t[:, sl], r=r_t[:, sl], bt=b_t[:, sl], kt=k_t[:, sl],
                                  bh=b_h[:, sl], kh=k_h[:, sl], v=v[:, sl], w_col=w_col)
        for ch in chains:
            o = ops[ch]
            m = _dot_nt(jnp.concatenate([o["a"], o["r"]], axis=0),
                        jnp.concatenate([bd(o["bt"]), bd(o["kt"])], axis=0))
            o["a_ab"] = jnp.where(strict, m[:c, :LANES], 0.0)
            o["a_ak"] = jnp.where(strict, m[:c, LANES:], 0.0)
            o["a_rb"] = jnp.where(incl, m[c:, :LANES], 0.0)
            o["a_rk"] = jnp.where(incl, m[c:, LANES:], 0.0)
        return ops

    def inverse_stages(ops):
        def first():
            for ch in chains:
                o = ops[ch]
                o["t"] = eye_w + o["a_ab"]
                o["p"] = _dot(o["a_ab"], bd(o["a_ab"]))

        def double(is_last):
            def run():
                for ch in chains:
                    o = ops[ch]
                    p_bd = bd(o["p"])
                    if is_last:
                        o["t"] = o["t"] + _dot(o["t"], p_bd)
                    else:
                        prod = _dot(jnp.concatenate([o["t"], o["p"]], axis=0), p_bd)
                        o["t"] = o["t"] + prod[:c]
                        o["p"] = prod[c:]
            return run

        steps = CHUNK.bit_length() - 2
        return [first] + [double(i == steps - 1) for i in range(steps)]

    def state_stages(ops, state, g):
        rows = slice(g * c, (g + 1) * c)

        def stage_x():
            for ch in chains:
                o = ops[ch]
                o["v_bd"] = bd(o["v"])
                o["x"] = _dot(jnp.concatenate([o["a"], o["a_ak"]], axis=1),
                              jnp.concatenate([state[ch], o["v_bd"]], axis=0))

        def stage_u():
            for ch in chains:
                o = ops[ch]
                o["u"] = _dot(o["t"], bd(o["x"]))

        def stage_s():
            for ch in chains:
                o = ops[ch]
                ds = _dot_tn(jnp.concatenate([o["bh"], o["kh"]], axis=0),
                             jnp.concatenate([o["u"], o["v"]], axis=0))
                o["s_new"] = o["w_col"] * state[ch] + jnp.where(same_head, ds, 0.0)

        def stage_y():
            for ch in chains:
                o = ops[ch]
                bi, p = ch
                y_ref[bi, rows, p * LANES:(p + 1) * LANES] = _dot(
                    jnp.concatenate([o["r"], o["a_rb"], o["a_rk"]], axis=1),
                    jnp.concatenate([state[ch], bd(o["u"]), o["v_bd"]], axis=0))
                state[ch] = o["s_new"]

        return [stage_x, stage_u, stage_s, stage_y]

    state = {ch: s_ref[ch[0], ch[1]] for ch in chains}
    ops_next = prep(0)
    pending = []
    for g in range(RWKV_CHUNKS):
        ops_cur = ops_next
        inv = inverse_stages(ops_cur)
        half = len(inv) // 2
        for i, stage in enumerate(inv):
            stage()
            if i < len(pending):
                pending[i]()
            if i == half and g + 1 < RWKV_CHUNKS:
                ops_next = prep(g + 1)
        for stage in pending[len(inv):]:
            stage()
        pending = state_stages(ops_cur, state, g)
    for stage in pending:
        stage()
    for ch, s_val in state.items():
        s_ref[ch[0], ch[1]] = s_val


def _rwkv(r, lw, k, v, kk, a, batch, seq):
    rows = RWKV_CHUNKS * CHUNK
    spec = pl.BlockSpec((batch, rows, RWKV_WIDTH), lambda i: (0, i, 0))
    shape3 = (batch, seq, RWKV_WIDTH)
    y = pl.pallas_call(
        _rwkv_kernel,
        grid=(seq // rows,),
        in_specs=[spec] * 6,
        out_specs=spec,
        out_shape=jax.ShapeDtypeStruct(shape3, F32),
        scratch_shapes=[pltpu.VMEM((batch, RWKV_WIDTH // LANES, LANES, LANES), F32)],
        compiler_params=pltpu.CompilerParams(
            dimension_semantics=("arbitrary",), vmem_limit_bytes=VMEM_LIMIT),
        name="rwkv",
    )(*(t.reshape(shape3) for t in (r, lw, k, v, kk, a)))
    return y.reshape(batch * seq, RWKV_WIDTH)


def _mixer_out_kernel(layer, tiles_per_seq, x_ref, y_ref, bonus_ref, g_ref, lru_ref, gnorm_ref, win_ref, prw_ref,
                      plr_ref, wout_ref, lnw_ref, lnb_ref, convw_ref, convb_ref, wax_ref, bax_ref, lam_ref,
                      hmean_ref, o_ref, xcarry, hcarry):
    tm = x_ref.shape[0]
    first = pl.program_id(0) % tiles_per_seq == 0
    row = lambda ref: ref[layer:layer + 1, :]

    @pl.when(first)
    def _():
        xcarry[...] = jnp.zeros(xcarry.shape, F32)
        hcarry[...] = jnp.zeros(hcarry.shape, F32)

    blocks = [slice(i * MIX_OUT_ROWS, (i + 1) * MIX_OUT_ROWS) for i in range(tm // MIX_OUT_ROWS)]
    st = [dict() for _ in blocks]
    lw_ = LRU_WIDTH
    for s, rs in zip(st, blocks):
        s["x"] = x_ref[rs, :]
        s["hx"] = _rmsnorm(s["x"], gnorm_ref[MIXER_NORM_ROW:MIXER_NORM_ROW + 1, :]).astype(BF16)
        s["y"] = y_ref[rs, :]
        s["mu"] = _head_dot(s["y"], hmean_ref)

    gate_jobs = [(s, name, W_MIX + j * D_MODEL) for s in st for j, name in enumerate(("gate_r", "gate_l"))]

    def gate_matmuls(count):
        for s, name, lo in gate_jobs[:count]:
            s[name] = jax.nn.sigmoid(jnp.dot(s["hx"], win_ref[:, lo:lo + D_MODEL], preferred_element_type=F32))
        del gate_jobs[:count]

    gate_matmuls(1)

    xprev = xcarry[...]
    for s, rs in zip(st, blocks):
        lx = lru_ref[rs, :lw_]
        xb = row(convb_ref) + convw_ref[CONV_WIDTH - 1:CONV_WIDTH, :] * lx
        for d in range(1, CONV_WIDTH):
            xb = xb + convw_ref[CONV_WIDTH - 1 - d:CONV_WIDTH - d, :] * _shift_rows(xprev, lx, d)
        xprev = lx[MIX_OUT_ROWS - SUBLANES:, :]
        s["xb"] = xb
        lg = [_dot(xb[:, g * LANES:(g + 1) * LANES], wax_ref[g]) for g in range(lw_ // LANES)]
        s["lgate_a"] = jnp.concatenate([t[:, :LANES] for t in lg], axis=1) + row(bax_ref)[:, :lw_]
        s["lgate_x"] = jnp.concatenate([t[:, LANES:] for t in lg], axis=1) + row(bax_ref)[:, lw_:]
    xcarry[...] = xprev
    gate_matmuls(1)

    for s, rs in zip(st, blocks):
        s["yc"] = s["y"] - s["mu"]
        s["var"] = _head_dot(s["yc"] * s["yc"], hmean_ref)
    gate_matmuls(1)
    for s, rs in zip(st, blocks):
        yn = s["yc"] * lax.rsqrt(s["var"] + GN_EPS) * row(lnw_ref) + row(lnb_ref) + bonus_ref[rs, :]
        s["rwkv_proj"] = _dot(yn * g_ref[rs, :], prw_ref[...])
    gate_matmuls(len(gate_jobs))

    h_prev = hcarry[SUBLANES - 1:SUBLANES, :]
    for i, (s, rs) in enumerate(zip(st, blocks)):
        gate_a = jax.nn.sigmoid(s["lgate_a"])
        gate_x = jax.nn.sigmoid(s["lgate_x"])
        log_a = -LRU_C * gate_a * _softplus(-row(lam_ref), accurate=True)
        a_l = jnp.exp(log_a)
        mult = jnp.sqrt(-jnp.tanh(log_a) * (a_l * a_l + 1.0))
        if i == 0:
            r_idx = lax.broadcasted_iota(jnp.int32, log_a.shape, 0)
            seq_start = r_idx + jnp.where(first, 0, 1) == 0
            mult = jnp.where(seq_start, 1.0, mult)
        hseq = _scan_rows(a_l, s["xb"] * gate_x * mult, h_prev)
        h_prev = hseq[MIX_OUT_ROWS - 1:, :]
        if i == len(blocks) - 1:
            hcarry[...] = hseq[MIX_OUT_ROWS - SUBLANES:, :]
        s["lru_proj"] = _dot(hseq * jax.nn.gelu(lru_ref[rs, lw_:]), plr_ref[...])

    for s, rs in zip(st, blocks):
        merged = s["gate_r"] * s["rwkv_proj"] + s["gate_l"] * s["lru_proj"]
        out = _dot(merged, wout_ref[...])
        o_ref[rs, :] = s["x"] + _rmsnorm(out, gnorm_ref[MIXER_NORM_ROW + 1:MIXER_NORM_ROW + 2, :])


def _mixer_out(x, seq, y, bonus, g, lru_in, l, norm_g, weights, prm, hmean):
    n = x.shape[0]
    tm = MIX_OUT_TILE
    row512 = pl.BlockSpec((tm, RWKV_WIDTH), lambda i: (i, 0))
    row1024 = pl.BlockSpec((tm, D_MODEL), lambda i: (i, 0))
    names = ["ln_w", "ln_b", "conv_w", "conv_b", "wax", "bax", "lru_lambda"]
    return pl.pallas_call(
        functools.partial(_mixer_out_kernel, l, seq // tm),
        grid=(n // tm,),
        in_specs=([row1024, row512, row512, row512, row1024, _layer(norm_g, l)]
                  + [_resident(w.shape) for w in weights]
                  + [_rows(prm[k]) if prm[k].ndim == 2 else _layer(prm[k], l) for k in names]
                  + [_full(hmean.shape)]),
        out_specs=row1024,
        out_shape=jax.ShapeDtypeStruct((n, D_MODEL), F32),
        scratch_shapes=[
            pltpu.VMEM((SUBLANES, LRU_WIDTH), F32),
            pltpu.VMEM((SUBLANES, LRU_WIDTH), F32),
        ],
        compiler_params=pltpu.CompilerParams(
            dimension_semantics=("arbitrary",), vmem_limit_bytes=VMEM_LIMIT),
        name="mixer_out",
    )(x, y, bonus, g, lru_in, norm_g, *weights, *[prm[k] for k in names], hmean)


def _block_diag(w):
    *lead, nb, d, e = w.shape
    eye = jnp.eye(nb, dtype=w.dtype)
    return jnp.einsum("...nde,nm->...ndme", w, eye).reshape(*lead, nb * d, nb * e)


def _lru_gate_weights(wa, wx):
    nl, nb, d, _ = wa.shape
    per = LANES // d
    pair = lambda w: _block_diag(w.reshape(nl, nb // per, per, d, d))
    return jnp.concatenate([pair(wa), pair(wx)], axis=-1)


def kernel(x, norm_g, ffn1_w_up, ffn1_w_down, ffn2_w_up, ffn2_w_down, w_in, shift_mu, w0, w_up, a0, a_up, g_up, k_k, k_a, r_k, ln_w, ln_b, v0, v_down, v_up, conv_w, conv_b, lru_wa, lru_ba, lru_wx, lru_bx, lru_lambda, p_rwkv, p_lru, w_out):
    batch, seq, d = x.shape
    depth = norm_g.shape[0]
    assert d == D_MODEL and seq % MIX_TILE == 0 and (batch * seq) % FFN_TILE == 0
    assert seq % (RWKV_CHUNKS * CHUNK) == 0 and seq % MIX_OUT_TILE == 0
    xf = x.reshape(batch * seq, d)
    bf = lambda t: t.astype(BF16)
    head = jnp.arange(MXU_COLS) // HEAD_DIM
    same = (head[:, None] == head[None, :])
    hsum = same.astype(BF16)
    hmean = (same.astype(F32) / HEAD_DIM).astype(BF16)
    zeros_lora = jnp.zeros((depth, DECAY_LORA, RWKV_WIDTH), F32)
    prm = dict(
        shift_mu=shift_mu, w0=w0, a0=a0, k_k=k_k, k_a=k_a, r_k=r_k,
        wup_pad=bf(jnp.concatenate([w_up, zeros_lora], axis=1)),
        aup_pad=bf(jnp.concatenate([zeros_lora, a_up], axis=1)),
        g_up=bf(g_up), v0=v0, v_down=bf(v_down), v_up=bf(v_up),
        conv_w=conv_w, conv_b=conv_b,
        wax=bf(_lru_gate_weights(lru_wa, lru_wx)),
        bax=jnp.concatenate([lru_ba, lru_bx], axis=1), lru_lambda=lru_lambda,
        ln_w=ln_w, ln_b=ln_b,
    )
    ffn1_b = (bf(ffn1_w_up[0]), bf(ffn1_w_down[0]))
    v_first = None
    for l in range(depth):
        xf, (w_in_b, p_rwkv_b, p_lru_b, w_out_b, *ffn2_b) = _ffn(
            xf, l, FFN1_NORM_ROW, norm_g, *ffn1_b,
            to_cast=[w_in, p_rwkv, p_lru, w_out, ffn2_w_up, ffn2_w_down], cast_layer=l)
        r, lw, k, v, kk, a, bonus, gate, lru_in = _mixer_in(xf, seq, l, norm_g, w_in_b, prm, v_first, hsum)
        if l == 0:
            v_first = v
        y = _rwkv(r, lw, k, v, kk, a, batch, seq)
        xf = _mixer_out(xf, seq, y, bonus, gate, lru_in, l, norm_g, (w_in_b, p_rwkv_b, p_lru_b, w_out_b),
                        prm, hmean)
        nxt = [ffn1_w_up, ffn1_w_down] if l + 1 < depth else []
        xf, ffn1_b = _ffn(xf, l, FFN2_NORM_ROW, norm_g, *ffn2_b, to_cast=nxt, cast_layer=l + 1)
    return xf.reshape(batch, seq, d)
```

```python
import functools

import jax
import jax.numpy as jnp
from jax import lax
from jax.experimental import pallas as pl
from jax.experimental.pallas import tpu as pltpu

D_MODEL = 1024
RWKV_WIDTH = 512
HEAD_DIM = 64
LRU_WIDTH = 512
CONV_WIDTH = 4
LRU_C = 8.0
D_FF = 2816
RMS_EPS = 1e-6
GN_EPS = 64e-5
FFN1_NORM_ROW = 0
MIXER_NORM_ROW = 2
FFN2_NORM_ROW = 4
DECAY_LORA = 64
AAA_LORA = 64
GATE_LORA = 128
W_SHIFT = 3 * RWKV_WIDTH + DECAY_LORA + AAA_LORA + GATE_LORA
W_MIX = W_SHIFT + 2 * LRU_WIDTH

RWKV_CHUNKS = 8
CHUNK = 64
LANES = 128
MXU_COLS = 256
SUBLANES = 8
BF16_ROWS = 16
FFN_TILE = 1024
FFN_ROWS = 512
FFN_COLS = 256
MIX_TILE = 512
MIX_OUT_TILE = 512
MIX_OUT_ROWS = 256
PROJ_COLS = 256
PROJ_SPLIT = (2, 2, 2, 5)
VMEM_LIMIT = 56 * 1024 * 1024

BF16 = jnp.bfloat16
F32 = jnp.float32


def _dot(a, b):
    return jnp.dot(a.astype(BF16), b.astype(BF16), preferred_element_type=F32)


def _dot_nt(a, b):
    return lax.dot_general(a.astype(BF16), b.astype(BF16), (((1,), (1,)), ((), ())),
                           preferred_element_type=F32)


def _dot_tn(a, b):
    return lax.dot_general(a.astype(BF16), b.astype(BF16), (((0,), (0,)), ((), ())),
                           preferred_element_type=F32)


def _head_dot(z, h_ref):
    w = h_ref[...]
    return jnp.concatenate(
        [_dot(z[:, i:i + MXU_COLS], w) for i in range(0, z.shape[1], MXU_COLS)], axis=1)


def _rmsnorm(x, g):
    ms = jnp.mean(x * x, axis=-1, keepdims=True)
    return x * lax.rsqrt(ms + RMS_EPS) * g


def _softplus(x, accurate):
    e = jnp.exp(-jnp.abs(x))
    return jnp.maximum(x, 0.0) + (jnp.log1p(e) if accurate else jnp.log(1.0 + e))


def _shift_rows(prev_rows, x, d):
    ext = jnp.concatenate([prev_rows, x], axis=0)
    return pltpu.roll(ext, d, 0)[SUBLANES:]


def _full(shape):
    return pl.BlockSpec(shape, lambda *_: (0,) * len(shape))


def _resident(shape):
    return pl.BlockSpec(shape, lambda *_: (0,) * len(shape), pipeline_mode=pl.Buffered(1))


def _layer(arr, l):
    shape = arr.shape[1:]
    return pl.BlockSpec((None,) + shape, lambda *_: (l,) + (0,) * len(shape),
                        pipeline_mode=pl.Buffered(1))


def _rows(arr):
    return pl.BlockSpec(arr.shape, lambda *_: (0, 0), pipeline_mode=pl.Buffered(1))


def _cast_job(src, l, steps):
    _, rows, cols = src.shape
    blk = next(b for b in range(BF16_ROWS, rows + 1, BF16_ROWS)
               if rows % b == 0 and b * steps >= rows)
    last = rows // blk - 1
    in_spec = pl.BlockSpec((None, blk, cols), lambda i: (l, jnp.minimum(i, last), 0))
    out_spec = pl.BlockSpec((blk, cols), lambda i: (jnp.minimum(i, last), 0))
    return in_spec, out_spec, jax.ShapeDtypeStruct((rows, cols), BF16)


def _ffn_kernel(norm_row, n_cast, x_ref, g_ref, wup_ref, wdn_ref, *refs):
    cast_src, o_ref, cast_dst = refs[:n_cast], refs[n_cast], refs[n_cast + 1:]
    for b in range(x_ref.shape[0] // FFN_ROWS):
        rs = slice(b * FFN_ROWS, (b + 1) * FFN_ROWS)
        x = x_ref[rs, :]
        h = _rmsnorm(x, g_ref[norm_row:norm_row + 1, :]).astype(BF16)
        acc = jnp.zeros(x.shape, F32)
        for j in range(D_FF // FFN_COLS):
            lo = j * FFN_COLS
            gate = jnp.dot(h, wup_ref[:, lo:lo + FFN_COLS], preferred_element_type=F32)
            up = jnp.dot(h, wup_ref[:, D_FF + lo:D_FF + lo + FFN_COLS], preferred_element_type=F32)
            act = (gate * jax.nn.sigmoid(gate) * up).astype(BF16)
            acc = acc + jnp.dot(act, wdn_ref[lo:lo + FFN_COLS, :], preferred_element_type=F32)
        o_ref[rs, :] = x + 0.5 * _rmsnorm(acc, g_ref[norm_row + 1:norm_row + 2, :])
    for src, dst in zip(cast_src, cast_dst):
        dst[...] = src[...].astype(BF16)


def _ffn(x, l, norm_row, norm_g, w_up, w_down, to_cast, cast_layer):
    n = x.shape[0]
    steps = n // FFN_TILE
    jobs = [_cast_job(src, cast_layer, steps) for src in to_cast]
    tile = pl.BlockSpec((FFN_TILE, D_MODEL), lambda i: (i, 0))
    out = pl.pallas_call(
        functools.partial(_ffn_kernel, norm_row, len(jobs)),
        grid=(steps,),
        in_specs=[tile, _layer(norm_g, l), _resident(w_up.shape), _resident(w_down.shape)]
        + [j[0] for j in jobs],
        out_specs=[tile] + [j[1] for j in jobs],
        out_shape=[jax.ShapeDtypeStruct((n, D_MODEL), F32)] + [j[2] for j in jobs],
        compiler_params=pltpu.CompilerParams(
            dimension_semantics=("arbitrary",), vmem_limit_bytes=VMEM_LIMIT),
        name="ffn",
    )(x, norm_g, w_up, w_down, *to_cast)
    return out[0], out[1:]


def _scan_rows(a, b, h0):
    n, w = a.shape
    groups = n // SUBLANES
    a = a.reshape(groups, SUBLANES, w)
    b = b.reshape(groups, SUBLANES, w)
    sub = lax.broadcasted_iota(jnp.int32, a.shape, 1)
    d = 1
    while d < SUBLANES:
        keep = sub >= d
        a_prev = jnp.where(keep, pltpu.roll(a, d, 1), 1.0)
        b_prev = jnp.where(keep, pltpu.roll(b, d, 1), 0.0)
        b = a * b_prev + b
        a = a * a_prev
        d *= 2
    out = []
    for i in range(groups):
        hi = a[i] * h0 + b[i]
        out.append(hi)
        h0 = hi[SUBLANES - 1:SUBLANES, :]
    return jnp.concatenate(out, axis=0)


def _mixer_in_kernel(layer, has_vmix, tiles_per_seq, *refs):
    (x_ref, g_ref, win_ref, mu_ref, w0_ref, wup_ref, a0_ref, aup_ref, gup_ref,
     kk_ref, ka_ref, rk_ref) = refs[:12]
    refs = refs[12:]
    if has_vmix:
        v0_ref, vdown_ref, vup_ref, vfirst_ref = refs[:4]
        refs = refs[4:]
    (hsum_ref, r_out, lw_out, k_out, v_out, kkn_out, a_out, bonus_out, g_out, lru_out,
     proj_even, proj_odd, pcarry) = refs

    tm = x_ref.shape[0]
    step = pl.program_id(0)
    row = lambda ref, l=layer: ref[l:l + 1, :]
    first = (step + tiles_per_seq - 1) % tiles_per_seq == 0

    @pl.when(step == 0)
    def _():
        proj_odd[...] = jnp.zeros(proj_odd.shape, F32)

    @pl.when(jnp.logical_or(first, step == 0))
    def _():
        pcarry[...] = jnp.zeros(pcarry.shape, F32)

    def body(proj_new, proj):
        h = _rmsnorm(x_ref[...], g_ref[MIXER_NORM_ROW:MIXER_NORM_ROW + 1, :]).astype(BF16)
        chunk_starts = iter(range(0, W_MIX, PROJ_COLS))

        def project(n_chunks):
            for _ in range(n_chunks):
                lo = next(chunk_starts)
                proj_new[:, lo:lo + PROJ_COLS] = jnp.dot(
                    h, win_ref[:, lo:lo + PROJ_COLS], preferred_element_type=F32)

        project(PROJ_SPLIT[0])
        p = proj[:, :W_SHIFT]
        prev = _shift_rows(pcarry[...], p, 1)
        pcarry[...] = p[tm - SUBLANES:, :]
        p = p + row(mu_ref) * (prev - p)
        rw = RWKV_WIDTH
        r, k, v = p[:, 0:rw], p[:, rw:2 * rw], p[:, 2 * rw:3 * rw]
        xwa = p[:, 3 * rw:3 * rw + DECAY_LORA + AAA_LORA]
        xg = p[:, 3 * rw + DECAY_LORA + AAA_LORA:]
        w = -_softplus(-(row(w0_ref) + _dot(jnp.tanh(xwa), wup_ref[...])), accurate=False) - 0.5
        a = jax.nn.sigmoid(row(a0_ref) + _dot(xwa, aup_ref[...]))
        g_out[...] = _dot(jax.nn.sigmoid(xg), gup_ref[...])
        if has_vmix:
            v_lora = _dot(v, vdown_ref[...])
        project(PROJ_SPLIT[1])
        lw_out[...] = -jnp.exp(w)
        if has_vmix:
            mix = jax.nn.sigmoid(row(v0_ref, layer - 1) + _dot(v_lora, vup_ref[...]))
            v = v + (vfirst_ref[...] - v) * mix
        kk = k * row(kk_ref)
        kk = kk * lax.rsqrt(jnp.maximum(_head_dot(kk * kk, hsum_ref), 1e-24))
        project(PROJ_SPLIT[2])
        k = k * (1.0 + (a - 1.0) * row(ka_ref))
        r_out[...] = r
        k_out[...] = k
        v_out[...] = v
        kkn_out[...] = kk
        a_out[...] = a
        bonus_out[...] = _head_dot(r * k * row(rk_ref), hsum_ref) * v
        project(PROJ_SPLIT[3])
        lru_out[...] = proj[:, W_SHIFT:]

    @pl.when(step % 2 == 0)
    def _():
        body(proj_even, proj_odd)

    @pl.when(step % 2 == 1)
    def _():
        body(proj_odd, proj_even)


def _mixer_in(x, seq, l, norm_g, w_in_b, prm, v_first, hsum):
    n = x.shape[0]
    tm = MIX_TILE
    tiles = n // tm
    has_vmix = v_first is not None
    x_spec = pl.BlockSpec((tm, D_MODEL), lambda i: (jnp.minimum(i, tiles - 1), 0))
    lag512 = pl.BlockSpec((tm, RWKV_WIDTH), lambda i: (jnp.maximum(i - 1, 0), 0))
    names = ["shift_mu", "w0", "wup_pad", "a0", "aup_pad", "g_up", "k_k", "k_a", "r_k"]
    spec = lambda k, layer: _rows(prm[k]) if prm[k].ndim == 2 else _layer(prm[k], layer)
    w_mix_spec = pl.BlockSpec((D_MODEL, W_MIX), lambda i: (0, 0), pipeline_mode=pl.Buffered(1))
    in_specs = [x_spec, _layer(norm_g, l), w_mix_spec] + [spec(k, l) for k in names]
    args = [x, norm_g, w_in_b] + [prm[k] for k in names]
    if has_vmix:
        vnames = ["v0", "v_down", "v_up"]
        in_specs += [spec(k, l - 1) for k in vnames] + [lag512]
        args += [prm[k] for k in vnames] + [v_first]
    in_specs += [_full(hsum.shape)]
    args += [hsum]
    out512 = jax.ShapeDtypeStruct((n, RWKV_WIDTH), F32)
    lag_lru = pl.BlockSpec((tm, 2 * LRU_WIDTH), lambda i: (jnp.maximum(i - 1, 0), 0))
    return pl.pallas_call(
        functools.partial(_mixer_in_kernel, l, has_vmix, seq // tm),
        grid=(tiles + 1,),
        in_specs=in_specs,
        out_specs=[lag512] * 8 + [lag_lru],
        out_shape=[out512] * 8 + [jax.ShapeDtypeStruct((n, 2 * LRU_WIDTH), F32)],
        scratch_shapes=[
            pltpu.VMEM((tm, W_MIX), F32),
            pltpu.VMEM((tm, W_MIX), F32),
            pltpu.VMEM((SUBLANES, W_SHIFT), F32),
        ],
        compiler_params=pltpu.CompilerParams(
            dimension_semantics=("arbitrary",), vmem_limit_bytes=VMEM_LIMIT),
        name="mixer_in",
    )(*args)


def _cumsum_rows(x):
    n, w = x.shape
    groups = n // SUBLANES
    x = x.reshape(groups, SUBLANES, w)
    sub = lax.broadcasted_iota(jnp.int32, x.shape, 1)
    d = 1
    while d < SUBLANES:
        x = x + jnp.where(sub >= d, pltpu.roll(x, d, 1), 0.0)
        d *= 2
    out = [x[0]]
    for i in range(1, groups):
        out.append(x[i] + out[-1][SUBLANES - 1:SUBLANES, :])
    return jnp.concatenate(out, axis=0)


def _rwkv_kernel(r_ref, lw_ref, k_ref, v_ref, kk_ref, a_ref, y_ref, s_ref):
    c = CHUNK
    nb = r_ref.shape[0]
    pairs = RWKV_WIDTH // LANES
    chains = [(bi, p) for bi in range(nb) for p in range(pairs)]

    @pl.when(pl.program_id(0) == 0)
    def _():
        s_ref[...] = jnp.zeros(s_ref.shape, F32)

    lane = lax.broadcasted_iota(jnp.int32, (c, LANES), 1)
    t_idx = lax.broadcasted_iota(jnp.int32, (c, LANES), 0)
    left = lane < HEAD_DIM
    s_idx = jnp.where(left, lane, lane - HEAD_DIM)
    strict = t_idx > s_idx
    incl = t_idx >= s_idx
    eye_w = (t_idx == s_idx).astype(F32)
    row2 = lax.broadcasted_iota(jnp.int32, (LANES, LANES), 0)
    lane2 = lax.broadcasted_iota(jnp.int32, (LANES, LANES), 1)
    same_head = (row2 < HEAD_DIM) == (lane2 < HEAD_DIM)
    eye2 = row2 == lane2

    def bd(w):
        return jnp.concatenate([jnp.where(left, w, 0.0), jnp.where(left, 0.0, w)], axis=0)

    def prep(g):
        ops = {}
        rows = slice(g * c, (g + 1) * c)
        for bi in range(nb):
            lw = lw_ref[bi, rows, :]
            cum = _cumsum_rows(lw)
            last = cum[c - 1:c, :]
            kk = kk_ref[bi, rows, :]
            k = k_ref[bi, rows, :]
            b = kk * a_ref[bi, rows, :]
            e_neg = jnp.exp(-cum)
            e_tail = jnp.exp(last - cum)
            a_t = -kk * jnp.exp(cum - lw)
            r_t = r_ref[bi, rows, :] * jnp.exp(cum)
            b_t = b * e_neg
            k_t = k * e_neg
            b_h = b * e_tail
            k_h = k * e_tail
            w_c = jnp.exp(last)
            v = v_ref[bi, rows, :]
            for p in range(pairs):
                sl = slice(p * LANES, (p + 1) * LANES)
                w_col = jnp.sum(jnp.where(eye2, w_c[:, sl], 0.0), axis=1, keepdims=True)
                ops[bi, p] = dict(a=a_t[:, sl], r=r_t[:, sl], bt=b_t[:, sl], kt=k_t[:, sl],
                                  bh=b_h[:, sl], kh=k_h[:, sl], v=v[:, sl], w_col=w_col)
        for ch in chains:
            o = ops[ch]
            m = _dot_nt(jnp.concatenate([o["a"], o["r"]], axis=0),
                        jnp.concatenate([bd(o["bt"]), bd(o["kt"])], axis=0))
            o["a_ab"] = jnp.where(strict, m[:c, :LANES], 0.0)
            o["a_ak"] = jnp.where(strict, m[:c, LANES:], 0.0)
            o["a_rb"] = jnp.where(incl, m[c:, :LANES], 0.0)
            o["a_rk"] = jnp.where(incl, m[c:, LANES:], 0.0)
        return ops

    def inverse_stages(ops):
        def first():
            for ch in chains:
                o = ops[ch]
                o["t"] = eye_w + o["a_ab"]
                o["p"] = _dot(o["a_ab"], bd(o["a_ab"]))

        def double(is_last):
            def run():
                for ch in chains:
                    o = ops[ch]
                    p_bd = bd(o["p"])
                    if is_last:
                        o["t"] = o["t"] + _dot(o["t"], p_bd)
                    else:
                        prod = _dot(jnp.concatenate([o["t"], o["p"]], axis=0), p_bd)
                        o["t"] = o["t"] + prod[:c]
                        o["p"] = prod[c:]
            return run

        steps = CHUNK.bit_length() - 2
        return [first] + [double(i == steps - 1) for i in range(steps)]

    def state_stages(ops, state, g):
        rows = slice(g * c, (g + 1) * c)

        def stage_x():
            for ch in chains:
                o = ops[ch]
                o["v_bd"] = bd(o["v"])
                o["x"] = _dot(jnp.concatenate([o["a"], o["a_ak"]], axis=1),
                              jnp.concatenate([state[ch], o["v_bd"]], axis=0))

        def stage_u():
            for ch in chains:
                o = ops[ch]
                o["u"] = _dot(o["t"], bd(o["x"]))

        def stage_s():
            for ch in chains:
                o = ops[ch]
                ds = _dot_tn(jnp.concatenate([o["bh"], o["kh"]], axis=0),
                             jnp.concatenate([o["u"], o["v"]], axis=0))
                o["s_new"] = o["w_col"] * state[ch] + jnp.where(same_head, ds, 0.0)

        def stage_y():
            for ch in chains:
                o = ops[ch]
                bi, p = ch
                y_ref[bi, rows, p * LANES:(p + 1) * LANES] = _dot(
                    jnp.concatenate([o["r"], o["a_rb"], o["a_rk"]], axis=1),
                    jnp.concatenate([state[ch], bd(o["u"]), o["v_bd"]], axis=0))
                state[ch] = o["s_new"]

        return [stage_x, stage_u, stage_s, stage_y]

    state = {ch: s_ref[ch[0], ch[1]] for ch in chains}
    ops_next = prep(0)
    pending = []
    for g in range(RWKV_CHUNKS):
        ops_cur = ops_next
        inv = inverse_stages(ops_cur)
        half = len(inv) // 2
        for i, stage in enumerate(inv):
            stage()
            if i < len(pending):
                pending[i]()
            if i == half and g + 1 < RWKV_CHUNKS:
                ops_next = prep(g + 1)
        for stage in pending[len(inv):]:
            stage()
        pending = state_stages(ops_cur, state, g)
    for stage in pending:
        stage()
    for ch, s_val in state.items():
        s_ref[ch[0], ch[1]] = s_val


def _rwkv(r, lw, k, v, kk, a, batch, seq):
    rows = RWKV_CHUNKS * CHUNK
    spec = pl.BlockSpec((batch, rows, RWKV_WIDTH), lambda i: (0, i, 0))
    shape3 = (batch, seq, RWKV_WIDTH)
    y = pl.pallas_call(
        _rwkv_kernel,
        grid=(seq // rows,),
        in_specs=[spec] * 6,
        out_specs=spec,
        out_shape=jax.ShapeDtypeStruct(shape3, F32),
        scratch_shapes=[pltpu.VMEM((batch, RWKV_WIDTH // LANES, LANES, LANES), F32)],
        compiler_params=pltpu.CompilerParams(
            dimension_semantics=("arbitrary",), vmem_limit_bytes=VMEM_LIMIT),
        name="rwkv",
    )(*(t.reshape(shape3) for t in (r, lw, k, v, kk, a)))
    return y.reshape(batch * seq, RWKV_WIDTH)


def _mixer_out_kernel(layer, tiles_per_seq, x_ref, y_ref, bonus_ref, g_ref, lru_ref, gnorm_ref, win_ref, prw_ref,
                      plr_ref, wout_ref, lnw_ref, lnb_ref, convw_ref, convb_ref, wax_ref, bax_ref, lam_ref,
                      hmean_ref, o_ref, xcarry, hcarry):
    tm = x_ref.shape[0]
    first = pl.program_id(0) % tiles_per_seq == 0
    row = lambda ref: ref[layer:layer + 1, :]

    @pl.when(first)
    def _():
        xcarry[...] = jnp.zeros(xcarry.shape, F32)
        hcarry[...] = jnp.zeros(hcarry.shape, F32)

    blocks = [slice(i * MIX_OUT_ROWS, (i + 1) * MIX_OUT_ROWS) for i in range(tm // MIX_OUT_ROWS)]
    st = [dict() for _ in blocks]
    lw_ = LRU_WIDTH
    for s, rs in zip(st, blocks):
        s["x"] = x_ref[rs, :]
        s["hx"] = _rmsnorm(s["x"], gnorm_ref[MIXER_NORM_ROW:MIXER_NORM_ROW + 1, :]).astype(BF16)
        s["y"] = y_ref[rs, :]
        s["mu"] = _head_dot(s["y"], hmean_ref)

    gate_jobs = [(s, name, W_MIX + j * D_MODEL) for s in st for j, name in enumerate(("gate_r", "gate_l"))]

    def gate_matmuls(count):
        for s, name, lo in gate_jobs[:count]:
            s[name] = jax.nn.sigmoid(jnp.dot(s["hx"], win_ref[:, lo:lo + D_MODEL], preferred_element_type=F32))
        del gate_jobs[:count]

    gate_matmuls(1)

    xprev = xcarry[...]
    for s, rs in zip(st, blocks):
        lx = lru_ref[rs, :lw_]
        xb = row(convb_ref) + convw_ref[CONV_WIDTH - 1:CONV_WIDTH, :] * lx
        for d in range(1, CONV_WIDTH):
            xb = xb + convw_ref[CONV_WIDTH - 1 - d:CONV_WIDTH - d, :] * _shift_rows(xprev, lx, d)
        xprev = lx[MIX_OUT_ROWS - SUBLANES:, :]
        s["xb"] = xb
        lg = [_dot(xb[:, g * LANES:(g + 1) * LANES], wax_ref[g]) for g in range(lw_ // LANES)]
        s["lgate_a"] = jnp.concatenate([t[:, :LANES] for t in lg], axis=1) + row(bax_ref)[:, :lw_]
        s["lgate_x"] = jnp.concatenate([t[:, LANES:] for t in lg], axis=1) + row(bax_ref)[:, lw_:]
    xcarry[...] = xprev
    gate_matmuls(1)

    for s, rs in zip(st, blocks):
        s["yc"] = s["y"] - s["mu"]
        s["var"] = _head_dot(s["yc"] * s["yc"], hmean_ref)
    gate_matmuls(1)
    for s, rs in zip(st, blocks):
        yn = s["yc"] * lax.rsqrt(s["var"] + GN_EPS) * row(lnw_ref) + row(lnb_ref) + bonus_ref[rs, :]
        s["rwkv_proj"] = _dot(yn * g_ref[rs, :], prw_ref[...])
    gate_matmuls(len(gate_jobs))

    h_prev = hcarry[SUBLANES - 1:SUBLANES, :]
    for i, (s, rs) in enumerate(zip(st, blocks)):
        gate_a = jax.nn.sigmoid(s["lgate_a"])
        gate_x = jax.nn.sigmoid(s["lgate_x"])
        log_a = -LRU_C * gate_a * _softplus(-row(lam_ref), accurate=True)
        a_l = jnp.exp(log_a)
        mult = jnp.sqrt(-jnp.tanh(log_a) * (a_l * a_l + 1.0))
        if i == 0:
            r_idx = lax.broadcasted_iota(jnp.int32, log_a.shape, 0)
            seq_start = r_idx + jnp.where(first, 0, 1) == 0
            mult = jnp.where(seq_start, 1.0, mult)
        hseq = _scan_rows(a_l, s["xb"] * gate_x * mult, h_prev)
        h_prev = hseq[MIX_OUT_ROWS - 1:, :]
        if i == len(blocks) - 1:
            hcarry[...] = hseq[MIX_OUT_ROWS - SUBLANES:, :]
        s["lru_proj"] = _dot(hseq * jax.nn.gelu(lru_ref[rs, lw_:]), plr_ref[...])

    for s, rs in zip(st, blocks):
        merged = s["gate_r"] * s["rwkv_proj"] + s["gate_l"] * s["lru_proj"]
        out = _dot(merged, wout_ref[...])
        o_ref[rs, :] = s["x"] + _rmsnorm(out, gnorm_ref[MIXER_NORM_ROW + 1:MIXER_NORM_ROW + 2, :])


def _mixer_out(x, seq, y, bonus, g, lru_in, l, norm_g, weights, prm, hmean):
    n = x.shape[0]
    tm = MIX_OUT_TILE
    row512 = pl.BlockSpec((tm, RWKV_WIDTH), lambda i: (i, 0))
    row1024 = pl.BlockSpec((tm, D_MODEL), lambda i: (i, 0))
    names = ["ln_w", "ln_b", "conv_w", "conv_b", "wax", "bax", "lru_lambda"]
    return pl.pallas_call(
        functools.partial(_mixer_out_kernel, l, seq // tm),
        grid=(n // tm,),
        in_specs=([row1024, row512, row512, row512, row1024, _layer(norm_g, l)]
                  + [_resident(w.shape) for w in weights]
                  + [_rows(prm[k]) if prm[k].ndim == 2 else _layer(prm[k], l) for k in names]
                  + [_full(hmean.shape)]),
        out_specs=row1024,
        out_shape=jax.ShapeDtypeStruct((n, D_MODEL), F32),
        scratch_shapes=[
            pltpu.VMEM((SUBLANES, LRU_WIDTH), F32),
            pltpu.VMEM((SUBLANES, LRU_WIDTH), F32),
        ],
        compiler_params=pltpu.CompilerParams(
            dimension_semantics=("arbitrary",), vmem_limit_bytes=VMEM_LIMIT),
        name="mixer_out",
    )(x, y, bonus, g, lru_in, norm_g, *weights, *[prm[k] for k in names], hmean)


def _block_diag(w):
    *lead, nb, d, e = w.shape
    eye = jnp.eye(nb, dtype=w.dtype)
    return jnp.einsum("...nde,nm->...ndme", w, eye).reshape(*lead, nb * d, nb * e)


def _lru_gate_weights(wa, wx):
    nl, nb, d, _ = wa.shape
    per = LANES // d
    pair = lambda w: _block_diag(w.reshape(nl, nb // per, per, d, d))
    return jnp.concatenate([pair(wa), pair(wx)], axis=-1)


def kernel(x, norm_g, ffn1_w_up, ffn1_w_down, ffn2_w_up, ffn2_w_down, w_in, shift_mu, w0, w_up, a0, a_up, g_up, k_k, k_a, r_k, ln_w, ln_b, v0, v_down, v_up, conv_w, conv_b, lru_wa, lru_ba, lru_wx, lru_bx, lru_lambda, p_rwkv, p_lru, w_out):
    batch, seq, d = x.shape
    depth = norm_g.shape[0]
    assert d == D_MODEL and seq % MIX_TILE == 0 and (batch * seq) % FFN_TILE == 0
    assert seq % (RWKV_CHUNKS * CHUNK) == 0 and seq % MIX_OUT_TILE == 0
    xf = x.reshape(batch * seq, d)
    bf = lambda t: t.astype(BF16)
    head = jnp.arange(MXU_COLS) // HEAD_DIM
    same = (head[:, None] == head[None, :])
    hsum = same.astype(BF16)
    hmean = (same.astype(F32) / HEAD_DIM).astype(BF16)
    zeros_lora = jnp.zeros((depth, DECAY_LORA, RWKV_WIDTH), F32)
    prm = dict(
        shift_mu=shift_mu, w0=w0, a0=a0, k_k=k_k, k_a=k_a, r_k=r_k,
        wup_pad=bf(jnp.concatenate([w_up, zeros_lora], axis=1)),
        aup_pad=bf(jnp.concatenate([zeros_lora, a_up], axis=1)),
        g_up=bf(g_up), v0=v0, v_down=bf(v_down), v_up=bf(v_up),
        conv_w=conv_w, conv_b=conv_b,
        wax=bf(_lru_gate_weights(lru_wa, lru_wx)),
        bax=jnp.concatenate([lru_ba, lru_bx], axis=1), lru_lambda=lru_lambda,
        ln_w=ln_w, ln_b=ln_b,
    )
    ffn1_b = (bf(ffn1_w_up[0]), bf(ffn1_w_down[0]))
    v_first = None
    for l in range(depth):
        xf, (w_in_b, p_rwkv_b, p_lru_b, w_out_b, *ffn2_b) = _ffn(
            xf, l, FFN1_NORM_ROW, norm_g, *ffn1_b,
            to_cast=[w_in, p_rwkv, p_lru, w_out, ffn2_w_up, ffn2_w_down], cast_layer=l)
        r, lw, k, v, kk, a, bonus, gate, lru_in = _mixer_in(xf, seq, l, norm_g, w_in_b, prm, v_first, hsum)
        if l == 0:
            v_first = v
        y = _rwkv(r, lw, k, v, kk, a, batch, seq)
        xf = _mixer_out(xf, seq, y, bonus, gate, lru_in, l, norm_g, (w_in_b, p_rwkv_b, p_lru_b, w_out_b),
                        prm, hmean)
        nxt = [ffn1_w_up, ffn1_w_down] if l + 1 < depth else []
        xf, ffn1_b = _ffn(xf, l, FFN2_NORM_ROW, norm_g, *ffn2_b, to_cast=nxt, cast_layer=l + 1)
    return xf.reshape(batch, seq, d)
```

```python
import functools

import jax
import jax.numpy as jnp
from jax import lax
from jax.experimental import pallas as pl
from jax.experimental.pallas import tpu as pltpu

D_MODEL = 1024
RWKV_WIDTH = 512
HEAD_DIM = 64
LRU_WIDTH = 512
CONV_WIDTH = 4
LRU_C = 8.0
D_FF = 2816
RMS_EPS = 1e-6
GN_EPS = 64e-5
FFN1_NORM_ROW = 0
MIXER_NORM_ROW = 2
FFN2_NORM_ROW = 4
DECAY_LORA = 64
AAA_LORA = 64
GATE_LORA = 128
W_SHIFT = 3 * RWKV_WIDTH + DECAY_LORA + AAA_LORA + GATE_LORA
W_MIX = W_SHIFT + 2 * LRU_WIDTH

RWKV_CHUNKS = 8
CHUNK = 64
LANES = 128
MXU_COLS = 256
SUBLANES = 8
BF16_ROWS = 16
FFN_TILE = 1024
FFN_ROWS = 512
FFN_COLS = 256
MIX_TILE = 512
MIX_OUT_TILE = 512
MIX_OUT_ROWS = 256
PROJ_COLS = 256
PROJ_SPLIT = (2, 2, 2, 5)
VMEM_LIMIT = 56 * 1024 * 1024

BF16 = jnp.bfloat16
F32 = jnp.float32


def _dot(a, b):
    return jnp.dot(a.astype(BF16), b.astype(BF16), preferred_element_type=F32)


def _dot_nt(a, b):
    return lax.dot_general(a.astype(BF16), b.astype(BF16), (((1,), (1,)), ((), ())),
                           preferred_element_type=F32)


def _dot_tn(a, b):
    return lax.dot_general(a.astype(BF16), b.astype(BF16), (((0,), (0,)), ((), ())),
                           preferred_element_type=F32)


def _head_dot(z, h_ref):
    w = h_ref[...]
    return jnp.concatenate(
        [_dot(z[:, i:i + MXU_COLS], w) for i in range(0, z.shape[1], MXU_COLS)], axis=1)


def _rmsnorm(x, g):
    ms = jnp.mean(x * x, axis=-1, keepdims=True)
    return x * lax.rsqrt(ms + RMS_EPS) * g


def _softplus(x, accurate):
    e = jnp.exp(-jnp.abs(x))
    return jnp.maximum(x, 0.0) + (jnp.log1p(e) if accurate else jnp.log(1.0 + e))


def _shift_rows(prev_rows, x, d):
    ext = jnp.concatenate([prev_rows, x], axis=0)
    return pltpu.roll(ext, d, 0)[SUBLANES:]


def _full(shape):
    return pl.BlockSpec(shape, lambda *_: (0,) * len(shape))


def _resident(shape):
    return pl.BlockSpec(shape, lambda *_: (0,) * len(shape), pipeline_mode=pl.Buffered(1))


def _layer(arr, l):
    shape = arr.shape[1:]
    return pl.BlockSpec((None,) + shape, lambda *_: (l,) + (0,) * len(shape),
                        pipeline_mode=pl.Buffered(1))


def _rows(arr):
    return pl.BlockSpec(arr.shape, lambda *_: (0, 0), pipeline_mode=pl.Buffered(1))


def _cast_job(src, l, steps):
    _, rows, cols = src.shape
    blk = next(b for b in range(BF16_ROWS, rows + 1, BF16_ROWS)
               if rows % b == 0 and b * steps >= rows)
    last = rows // blk - 1
    in_spec = pl.BlockSpec((None, blk, cols), lambda i: (l, jnp.minimum(i, last), 0))
    out_spec = pl.BlockSpec((blk, cols), lambda i: (jnp.minimum(i, last), 0))
    return in_spec, out_spec, jax.ShapeDtypeStruct((rows, cols), BF16)


def _ffn_kernel(norm_row, n_cast, x_ref, g_ref, wup_ref, wdn_ref, *refs):
    cast_src, o_ref, cast_dst = refs[:n_cast], refs[n_cast], refs[n_cast + 1:]
    for b in range(x_ref.shape[0] // FFN_ROWS):
        rs = slice(b * FFN_ROWS, (b + 1) * FFN_ROWS)
        x = x_ref[rs, :]
        h = _rmsnorm(x, g_ref[norm_row:norm_row + 1, :]).astype(BF16)
        acc = jnp.zeros(x.shape, F32)
        for j in range(D_FF // FFN_COLS):
            lo = j * FFN_COLS
            gate = jnp.dot(h, wup_ref[:, lo:lo + FFN_COLS], preferred_element_type=F32)
            up = jnp.dot(h, wup_ref[:, D_FF + lo:D_FF + lo + FFN_COLS], preferred_element_type=F32)
            act = (gate * jax.nn.sigmoid(gate) * up).astype(BF16)
            acc = acc + jnp.dot(act, wdn_ref[lo:lo + FFN_COLS, :], preferred_element_type=F32)
        o_ref[rs, :] = x + _rmsnorm(acc, 0.5 * g_ref[norm_row + 1:norm_row + 2, :])
    for src, dst in zip(cast_src, cast_dst):
        dst[...] = src[...].astype(BF16)


def _ffn(x, l, norm_row, norm_g, w_up, w_down, to_cast, cast_layer):
    n = x.shape[0]
    steps = n // FFN_TILE
    jobs = [_cast_job(src, cast_layer, steps) for src in to_cast]
    tile = pl.BlockSpec((FFN_TILE, D_MODEL), lambda i: (i, 0))
    out = pl.pallas_call(
        functools.partial(_ffn_kernel, norm_row, len(jobs)),
        grid=(steps,),
        in_specs=[tile, _layer(norm_g, l), _resident(w_up.shape), _resident(w_down.shape)]
        + [j[0] for j in jobs],
        out_specs=[tile] + [j[1] for j in jobs],
        out_shape=[jax.ShapeDtypeStruct((n, D_MODEL), F32)] + [j[2] for j in jobs],
        compiler_params=pltpu.CompilerParams(
            dimension_semantics=("arbitrary",), vmem_limit_bytes=VMEM_LIMIT),
        name="ffn",
    )(x, norm_g, w_up, w_down, *to_cast)
    return out[0], out[1:]


def _scan_rows(a, b, h0):
    n, w = a.shape
    groups = n // SUBLANES
    a = a.reshape(groups, SUBLANES, w)
    b = b.reshape(groups, SUBLANES, w)
    sub = lax.broadcasted_iota(jnp.int32, a.shape, 1)
    d = 1
    while d < SUBLANES:
        keep = sub >= d
        a_prev = jnp.where(keep, pltpu.roll(a, d, 1), 1.0)
        b_prev = jnp.where(keep, pltpu.roll(b, d, 1), 0.0)
        b = a * b_prev + b
        a = a * a_prev
        d *= 2
    out = []
    for i in range(groups):
        hi = a[i] * h0 + b[i]
        out.append(hi)
        h0 = hi[SUBLANES - 1:SUBLANES, :]
    return jnp.concatenate(out, axis=0)


def _mixer_in_kernel(layer, has_vmix, tiles_per_seq, *refs):
    (x_ref, g_ref, win_ref, mu_ref, w0_ref, wup_ref, a0_ref, aup_ref, gup_ref,
     kk_ref, ka_ref, rk_ref) = refs[:12]
    refs = refs[12:]
    if has_vmix:
        v0_ref, vdown_ref, vup_ref, vfirst_ref = refs[:4]
        refs = refs[4:]
    (hsum_ref, r_out, lw_out, k_out, v_out, kkn_out, a_out, bonus_out, g_out, lru_out,
     proj_even, proj_odd, pcarry) = refs

    tm = x_ref.shape[0]
    step = pl.program_id(0)
    row = lambda ref, l=layer: ref[l:l + 1, :]
    first = (step + tiles_per_seq - 1) % tiles_per_seq == 0

    @pl.when(step == 0)
    def _():
        proj_odd[...] = jnp.zeros(proj_odd.shape, F32)

    @pl.when(jnp.logical_or(first, step == 0))
    def _():
        pcarry[...] = jnp.zeros(pcarry.shape, F32)

    def body(proj_new, proj):
        h = _rmsnorm(x_ref[...], g_ref[MIXER_NORM_ROW:MIXER_NORM_ROW + 1, :]).astype(BF16)
        chunk_starts = iter(range(0, W_MIX, PROJ_COLS))

        def project(n_chunks):
            for _ in range(n_chunks):
                lo = next(chunk_starts)
                proj_new[:, lo:lo + PROJ_COLS] = jnp.dot(
                    h, win_ref[:, lo:lo + PROJ_COLS], preferred_element_type=F32)

        project(PROJ_SPLIT[0])
        p = proj[:, :W_SHIFT]
        prev = _shift_rows(pcarry[...], p, 1)
        pcarry[...] = p[tm - SUBLANES:, :]
        p = p + row(mu_ref) * (prev - p)
        rw = RWKV_WIDTH
        r, k, v = p[:, 0:rw], p[:, rw:2 * rw], p[:, 2 * rw:3 * rw]
        xwa = p[:, 3 * rw:3 * rw + DECAY_LORA + AAA_LORA]
        xg = p[:, 3 * rw + DECAY_LORA + AAA_LORA:]
        w = -_softplus(-(row(w0_ref) + _dot(jnp.tanh(xwa), wup_ref[...])), accurate=False) - 0.5
        a = jax.nn.sigmoid(row(a0_ref) + _dot(xwa, aup_ref[...]))
        g_out[...] = _dot(jax.nn.sigmoid(xg), gup_ref[...])
        if has_vmix:
            v_lora = _dot(v, vdown_ref[...])
        project(PROJ_SPLIT[1])
        lw_out[...] = -jnp.exp(w)
        if has_vmix:
            mix = jax.nn.sigmoid(row(v0_ref, layer - 1) + _dot(v_lora, vup_ref[...]))
            v = v + (vfirst_ref[...] - v) * mix
        kk = k * row(kk_ref)
        kk = kk * lax.rsqrt(jnp.maximum(_head_dot(kk * kk, hsum_ref), 1e-24))
        project(PROJ_SPLIT[2])
        k = k * (1.0 + (a - 1.0) * row(ka_ref))
        r_out[...] = r
        k_out[...] = k
        v_out[...] = v
        kkn_out[...] = kk
        a_out[...] = a
        bonus_out[...] = _head_dot(r * k * row(rk_ref), hsum_ref) * v
        project(PROJ_SPLIT[3])
        lru_out[...] = proj[:, W_SHIFT:]

    @pl.when(step % 2 == 0)
    def _():
        body(proj_even, proj_odd)

    @pl.when(step % 2 == 1)
    def _():
        body(proj_odd, proj_even)


def _mixer_in(x, seq, l, norm_g, w_in_b, prm, v_first, hsum):
    n = x.shape[0]
    tm = MIX_TILE
    tiles = n // tm
    has_vmix = v_first is not None
    x_spec = pl.BlockSpec((tm, D_MODEL), lambda i: (jnp.minimum(i, tiles - 1), 0))
    lag512 = pl.BlockSpec((tm, RWKV_WIDTH), lambda i: (jnp.maximum(i - 1, 0), 0))
    names = ["shift_mu", "w0", "wup_pad", "a0", "aup_pad", "g_up", "k_k", "k_a", "r_k"]
    spec = lambda k, layer: _rows(prm[k]) if prm[k].ndim == 2 else _layer(prm[k], layer)
    w_mix_spec = pl.BlockSpec((D_MODEL, W_MIX), lambda i: (0, 0), pipeline_mode=pl.Buffered(1))
    in_specs = [x_spec, _layer(norm_g, l), w_mix_spec] + [spec(k, l) for k in names]
    args = [x, norm_g, w_in_b] + [prm[k] for k in names]
    if has_vmix:
        vnames = ["v0", "v_down", "v_up"]
        in_specs += [spec(k, l - 1) for k in vnames] + [lag512]
        args += [prm[k] for k in vnames] + [v_first]
    in_specs += [_full(hsum.shape)]
    args += [hsum]
    out512 = jax.ShapeDtypeStruct((n, RWKV_WIDTH), F32)
    lag_lru = pl.BlockSpec((tm, 2 * LRU_WIDTH), lambda i: (jnp.maximum(i - 1, 0), 0))
    return pl.pallas_call(
        functools.partial(_mixer_in_kernel, l, has_vmix, seq // tm),
        grid=(tiles + 1,),
        in_specs=in_specs,
        out_specs=[lag512] * 8 + [lag_lru],
        out_shape=[out512] * 8 + [jax.ShapeDtypeStruct((n, 2 * LRU_WIDTH), F32)],
        scratch_shapes=[
            pltpu.VMEM((tm, W_MIX), F32),
            pltpu.VMEM((tm, W_MIX), F32),
            pltpu.VMEM((SUBLANES, W_SHIFT), F32),
        ],
        compiler_params=pltpu.CompilerParams(
            dimension_semantics=("arbitrary",), vmem_limit_bytes=VMEM_LIMIT),
        name="mixer_in",
    )(*args)


def _cumsum_rows(x):
    n, w = x.shape
    groups = n // SUBLANES
    x = x.reshape(groups, SUBLANES, w)
    sub = lax.broadcasted_iota(jnp.int32, x.shape, 1)
    d = 1
    while d < SUBLANES:
        x = x + jnp.where(sub >= d, pltpu.roll(x, d, 1), 0.0)
        d *= 2
    out = [x[0]]
    for i in range(1, groups):
        out.append(x[i] + out[-1][SUBLANES - 1:SUBLANES, :])
    return jnp.concatenate(out, axis=0)


def _rwkv_kernel(r_ref, lw_ref, k_ref, v_ref, kk_ref, a_ref, y_ref, s_ref):
    c = CHUNK
    nb = r_ref.shape[0]
    pairs = RWKV_WIDTH // LANES
    chains = [(bi, p) for bi in range(nb) for p in range(pairs)]

    @pl.when(pl.program_id(0) == 0)
    def _():
        s_ref[...] = jnp.zeros(s_ref.shape, F32)

    lane = lax.broadcasted_iota(jnp.int32, (c, LANES), 1)
    t_idx = lax.broadcasted_iota(jnp.int32, (c, LANES), 0)
    left = lane < HEAD_DIM
    s_idx = jnp.where(left, lane, lane - HEAD_DIM)
    strict = t_idx > s_idx
    incl = t_idx >= s_idx
    eye_w = (t_idx == s_idx).astype(F32)
    row2 = lax.broadcasted_iota(jnp.int32, (LANES, LANES), 0)
    lane2 = lax.broadcasted_iota(jnp.int32, (LANES, LANES), 1)
    same_head = (row2 < HEAD_DIM) == (lane2 < HEAD_DIM)
    eye2 = row2 == lane2

    def bd(w):
        return jnp.concatenate([jnp.where(left, w, 0.0), jnp.where(left, 0.0, w)], axis=0)

    def prep(g):
        ops = {}
        rows = slice(g * c, (g + 1) * c)
        for bi in range(nb):
            lw = lw_ref[bi, rows, :]
            cum = _cumsum_rows(lw)
            last = cum[c - 1:c, :]
            kk = kk_ref[bi, rows, :]
            k = k_ref[bi, rows, :]
            b = kk * a_ref[bi, rows, :]
            e_neg = jnp.exp(-cum)
            e_tail = jnp.exp(last - cum)
            a_t = -kk * jnp.exp(cum - lw)
            r_t = r_ref[bi, rows, :] * jnp.exp(cum)
            b_t = b * e_neg
            k_t = k * e_neg
            b_h = b * e_tail
            k_h = k * e_tail
            w_c = jnp.exp(last)
            v = v_ref[bi, rows, :]
            for p in range(pairs):
                sl = slice(p * LANES, (p + 1) * LANES)
                w_col = jnp.sum(jnp.where(eye2, w_c[:, sl], 0.0), axis=1, keepdims=True)
                ops[bi, p] = dict(a=a_t[:, sl], r=r_t[:, sl], bt=b_t[:, sl], kt=k_t[:, sl],
                                  bh=b_h[:, sl], kh=k_h[:, sl], v=v[:, sl], w_col=w_col)
        for ch in chains:
            o = ops[ch]
            m = _dot_nt(jnp.concatenate([o["a"], o["r"]], axis=0),
                        jnp.concatenate([bd(o["bt"]), bd(o["kt"])], axis=0))
            o["a_ab"] = jnp.where(strict, m[:c, :LANES], 0.0)
            o["a_ak"] = jnp.where(strict, m[:c, LANES:], 0.0)
            o["a_rb"] = jnp.where(incl, m[c:, :LANES], 0.0)
            o["a_rk"] = jnp.where(incl, m[c:, LANES:], 0.0)
        return ops

    def inverse_stages(ops):
        def first():
            for ch in chains:
                o = ops[ch]
                o["t"] = eye_w + o["a_ab"]
                o["p"] = _dot(o["a_ab"], bd(o["a_ab"]))

        def double(is_last):
            def run():
                for ch in chains:
                    o = ops[ch]
                    p_bd = bd(o["p"])
                    if is_last:
                        o["t"] = o["t"] + _dot(o["t"], p_bd)
                    else:
                        prod = _dot(jnp.concatenate([o["t"], o["p"]], axis=0), p_bd)
                        o["t"] = o["t"] + prod[:c]
                        o["p"] = prod[c:]
            return run

        steps = CHUNK.bit_length() - 2
        return [first] + [double(i == steps - 1) for i in range(steps)]

    def state_stages(ops, state, g):
        rows = slice(g * c, (g + 1) * c)

        def stage_x():
            for ch in chains:
                o = ops[ch]
                o["v_bd"] = bd(o["v"])
                o["x"] = _dot(jnp.concatenate([o["a"], o["a_ak"]], axis=1),
                              jnp.concatenate([state[ch], o["v_bd"]], axis=0))

        def stage_u():
            for ch in chains:
                o = ops[ch]
                o["u"] = _dot(o["t"], bd(o["x"]))

        def stage_s():
            for ch in chains:
                o = ops[ch]
                ds = _dot_tn(jnp.concatenate([o["bh"], o["kh"]], axis=0),
                             jnp.concatenate([o["u"], o["v"]], axis=0))
                o["s_new"] = o["w_col"] * state[ch] + jnp.where(same_head, ds, 0.0)

        def stage_y():
            for ch in chains:
                o = ops[ch]
                bi, p = ch
                y_ref[bi, rows, p * LANES:(p + 1) * LANES] = _dot(
                    jnp.concatenate([o["r"], o["a_rb"], o["a_rk"]], axis=1),
                    jnp.concatenate([state[ch], bd(o["u"]), o["v_bd"]], axis=0))
                state[ch] = o["s_new"]

        return [stage_x, stage_u, stage_s, stage_y]

    state = {ch: s_ref[ch[0], ch[1]] for ch in chains}
    ops_next = prep(0)
    pending = []
    for g in range(RWKV_CHUNKS):
        ops_cur = ops_next
        inv = inverse_stages(ops_cur)
        half = len(inv) // 2
        for i, stage in enumerate(inv):
            stage()
            if i < len(pending):
                pending[i]()
            if i == half and g + 1 < RWKV_CHUNKS:
                ops_next = prep(g + 1)
        for stage in pending[len(inv):]:
            stage()
        pending = state_stages(ops_cur, state, g)
    for stage in pending:
        stage()
    for ch, s_val in state.items():
        s_ref[ch[0], ch[1]] = s_val


def _rwkv(r, lw, k, v, kk, a, batch, seq):
    rows = RWKV_CHUNKS * CHUNK
    spec = pl.BlockSpec((batch, rows, RWKV_WIDTH), lambda i: (0, i, 0))
    shape3 = (batch, seq, RWKV_WIDTH)
    y = pl.pallas_call(
        _rwkv_kernel,
        grid=(seq // rows,),
        in_specs=[spec] * 6,
        out_specs=spec,
        out_shape=jax.ShapeDtypeStruct(shape3, F32),
        scratch_shapes=[pltpu.VMEM((batch, RWKV_WIDTH // LANES, LANES, LANES), F32)],
        compiler_params=pltpu.CompilerParams(
            dimension_semantics=("arbitrary",), vmem_limit_bytes=VMEM_LIMIT),
        name="rwkv",
    )(*(t.reshape(shape3) for t in (r, lw, k, v, kk, a)))
    return y.reshape(batch * seq, RWKV_WIDTH)


def _mixer_out_kernel(layer, tiles_per_seq, x_ref, y_ref, bonus_ref, g_ref, lru_ref, gnorm_ref, win_ref, prw_ref,
                      plr_ref, wout_ref, lnw_ref, lnb_ref, convw_ref, convb_ref, wax_ref, bax_ref, lam_ref,
                      hmean_ref, o_ref, xcarry, hcarry):
    tm = x_ref.shape[0]
    first = pl.program_id(0) % tiles_per_seq == 0
    row = lambda ref: ref[layer:layer + 1, :]

    @pl.when(first)
    def _():
        xcarry[...] = jnp.zeros(xcarry.shape, F32)
        hcarry[...] = jnp.zeros(hcarry.shape, F32)

    blocks = [slice(i * MIX_OUT_ROWS, (i + 1) * MIX_OUT_ROWS) for i in range(tm // MIX_OUT_ROWS)]
    st = [dict() for _ in blocks]
    lw_ = LRU_WIDTH
    for s, rs in zip(st, blocks):
        s["x"] = x_ref[rs, :]
        s["hx"] = _rmsnorm(s["x"], gnorm_ref[MIXER_NORM_ROW:MIXER_NORM_ROW + 1, :]).astype(BF16)
        s["y"] = y_ref[rs, :]
        s["mu"] = _head_dot(s["y"], hmean_ref)

    gate_jobs = [(s, name, W_MIX + j * D_MODEL) for s in st for j, name in enumerate(("gate_r", "gate_l"))]

    def gate_matmuls(count):
        for s, name, lo in gate_jobs[:count]:
            s[name] = jax.nn.sigmoid(jnp.dot(s["hx"], win_ref[:, lo:lo + D_MODEL], preferred_element_type=F32))
        del gate_jobs[:count]

    gate_matmuls(1)

    xprev = xcarry[...]
    for s, rs in zip(st, blocks):
        lx = lru_ref[rs, :lw_]
        xb = row(convb_ref) + convw_ref[CONV_WIDTH - 1:CONV_WIDTH, :] * lx
        for d in range(1, CONV_WIDTH):
            xb = xb + convw_ref[CONV_WIDTH - 1 - d:CONV_WIDTH - d, :] * _shift_rows(xprev, lx, d)
        xprev = lx[MIX_OUT_ROWS - SUBLANES:, :]
        s["xb"] = xb
        lg = [_dot(xb[:, g * LANES:(g + 1) * LANES], wax_ref[g]) for g in range(lw_ // LANES)]
        s["lgate_a"] = jnp.concatenate([t[:, :LANES] for t in lg], axis=1) + row(bax_ref)[:, :lw_]
        s["lgate_x"] = jnp.concatenate([t[:, LANES:] for t in lg], axis=1) + row(bax_ref)[:, lw_:]
    xcarry[...] = xprev
    gate_matmuls(1)

    for s, rs in zip(st, blocks):
        s["yc"] = s["y"] - s["mu"]
        s["var"] = _head_dot(s["yc"] * s["yc"], hmean_ref)
    gate_matmuls(1)
    for s, rs in zip(st, blocks):
        yn = s["yc"] * lax.rsqrt(s["var"] + GN_EPS) * row(lnw_ref) + row(lnb_ref) + bonus_ref[rs, :]
        s["rwkv_proj"] = _dot(yn * g_ref[rs, :], prw_ref[...])
    gate_matmuls(len(gate_jobs))

    h_prev = hcarry[SUBLANES - 1:SUBLANES, :]
    for i, (s, rs) in enumerate(zip(st, blocks)):
        gate_a = jax.nn.sigmoid(s["lgate_a"])
        gate_x = jax.nn.sigmoid(s["lgate_x"])
        log_a = -LRU_C * gate_a * _softplus(-row(lam_ref), accurate=True)
        a_l = jnp.exp(log_a)
        mult = jnp.sqrt(-jnp.tanh(log_a) * (a_l * a_l + 1.0))
        if i == 0:
            r_idx = lax.broadcasted_iota(jnp.int32, log_a.shape, 0)
            seq_start = r_idx + jnp.where(first, 0, 1) == 0
            mult = jnp.where(seq_start, 1.0, mult)
        hseq = _scan_rows(a_l, s["xb"] * gate_x * mult, h_prev)
        h_prev = hseq[MIX_OUT_ROWS - 1:, :]
        if i == len(blocks) - 1:
            hcarry[...] = hseq[MIX_OUT_ROWS - SUBLANES:, :]
        s["lru_proj"] = _dot(hseq * jax.nn.gelu(lru_ref[rs, lw_:]), plr_ref[...])

    for s, rs in zip(st, blocks):
        merged = s["gate_r"] * s["rwkv_proj"] + s["gate_l"] * s["lru_proj"]
        out = _dot(merged, wout_ref[...])
        o_ref[rs, :] = s["x"] + _rmsnorm(out, gnorm_ref[MIXER_NORM_ROW + 1:MIXER_NORM_ROW + 2, :])


def _mixer_out(x, seq, y, bonus, g, lru_in, l, norm_g, weights, prm, hmean):
    n = x.shape[0]
    tm = MIX_OUT_TILE
    row512 = pl.BlockSpec((tm, RWKV_WIDTH), lambda i: (i, 0))
    row1024 = pl.BlockSpec((tm, D_MODEL), lambda i: (i, 0))
    names = ["ln_w", "ln_b", "conv_w", "conv_b", "wax", "bax", "lru_lambda"]
    return pl.pallas_call(
        functools.partial(_mixer_out_kernel, l, seq // tm),
        grid=(n // tm,),
        in_specs=([row1024, row512, row512, row512, row1024, _layer(norm_g, l)]
                  + [_resident(w.shape) for w in weights]
                  + [_rows(prm[k]) if prm[k].ndim == 2 else _layer(prm[k], l) for k in names]
                  + [_full(hmean.shape)]),
        out_specs=row1024,
        out_shape=jax.ShapeDtypeStruct((n, D_MODEL), F32),
        scratch_shapes=[
            pltpu.VMEM((SUBLANES, LRU_WIDTH), F32),
            pltpu.VMEM((SUBLANES, LRU_WIDTH), F32),
        ],
        compiler_params=pltpu.CompilerParams(
            dimension_semantics=("arbitrary",), vmem_limit_bytes=VMEM_LIMIT),
        name="mixer_out",
    )(x, y, bonus, g, lru_in, norm_g, *weights, *[prm[k] for k in names], hmean)


def _block_diag(w):
    *lead, nb, d, e = w.shape
    eye = jnp.eye(nb, dtype=w.dtype)
    return jnp.einsum("...nde,nm->...ndme", w, eye).reshape(*lead, nb * d, nb * e)


def _lru_gate_weights(wa, wx):
    nl, nb, d, _ = wa.shape
    per = LANES // d
    pair = lambda w: _block_diag(w.reshape(nl, nb // per, per, d, d))
    return jnp.concatenate([pair(wa), pair(wx)], axis=-1)


def kernel(x, norm_g, ffn1_w_up, ffn1_w_down, ffn2_w_up, ffn2_w_down, w_in, shift_mu, w0, w_up, a0, a_up, g_up, k_k, k_a, r_k, ln_w, ln_b, v0, v_down, v_up, conv_w, conv_b, lru_wa, lru_ba, lru_wx, lru_bx, lru_lambda, p_rwkv, p_lru, w_out):
    batch, seq, d = x.shape
    depth = norm_g.shape[0]
    assert d == D_MODEL and seq % MIX_TILE == 0 and (batch * seq) % FFN_TILE == 0
    assert seq % (RWKV_CHUNKS * CHUNK) == 0 and seq % MIX_OUT_TILE == 0
    xf = x.reshape(batch * seq, d)
    bf = lambda t: t.astype(BF16)
    head = jnp.arange(MXU_COLS) // HEAD_DIM
    same = (head[:, None] == head[None, :])
    hsum = same.astype(BF16)
    hmean = (same.astype(F32) / HEAD_DIM).astype(BF16)
    zeros_lora = jnp.zeros((depth, DECAY_LORA, RWKV_WIDTH), F32)
    prm = dict(
        shift_mu=shift_mu, w0=w0, a0=a0, k_k=k_k, k_a=k_a, r_k=r_k,
        wup_pad=bf(jnp.concatenate([w_up, zeros_lora], axis=1)),
        aup_pad=bf(jnp.concatenate([zeros_lora, a_up], axis=1)),
        g_up=bf(g_up), v0=v0, v_down=bf(v_down), v_up=bf(v_up),
        conv_w=conv_w, conv_b=conv_b,
        wax=bf(_lru_gate_weights(lru_wa, lru_wx)),
        bax=jnp.concatenate([lru_ba, lru_bx], axis=1), lru_lambda=lru_lambda,
        ln_w=ln_w, ln_b=ln_b,
    )
    ffn1_b = (bf(ffn1_w_up[0]), bf(ffn1_w_down[0]))
    v_first = None
    for l in range(depth):
        xf, (w_in_b, p_rwkv_b, p_lru_b, w_out_b, *ffn2_b) = _ffn(
            xf, l, FFN1_NORM_ROW, norm_g, *ffn1_b,
            to_cast=[w_in, p_rwkv, p_lru, w_out, ffn2_w_up, ffn2_w_down], cast_layer=l)
        r, lw, k, v, kk, a, bonus, gate, lru_in = _mixer_in(xf, seq, l, norm_g, w_in_b, prm, v_first, hsum)
        if l == 0:
            v_first = v
        y = _rwkv(r, lw, k, v, kk, a, batch, seq)
        xf = _mixer_out(xf, seq, y, bonus, gate, lru_in, l, norm_g, (w_in_b, p_rwkv_b, p_lru_b, w_out_b),
                        prm, hmean)
        nxt = [ffn1_w_up, ffn1_w_down] if l + 1 < depth else []
        xf, ffn1_b = _ffn(xf, l, FFN2_NORM_ROW, norm_g, *ffn2_b, to_cast=nxt, cast_layer=l + 1)
    return xf.reshape(batch, seq, d)
```
